```python
import math
import jax, jax.numpy as jnp
from jax import lax
import numpy as np

D_MODEL = 1024
BATCH = 16
SEQ = 4096
DEPTH = 2

GRID_W = 64
RMS_EPS = 1e-6
NEG_INF = -1e30
N_BRANCH = 4
BRANCH_W = 256

DIL_CFG = ((128, 1), (512, 4), (2048, 16))
N_DIL = 3
A_HEADS = 4
A_HEAD_DIM = 64
B_HEADS = 4
B_Q_RANK = 256
B_KV_RANK = 128
B_NOPE_DIM = 64
B_ROPE_DIM = 32
B_V_DIM = 64
ROPE_THETA = 10000.0
Q_BLOCK = 128
C_HEADS = 4
C_HEAD_DIM = 64
NA_ROWS = 8
NA_COLS = 16
NA_KEY_COLS = 2 * NA_COLS
D_Q_HEADS = 4
D_KV_HEADS = 2
D_HEAD_DIM = 64
D_RADIUS = 128
T5_BUCKETS = 32
T5_MAX_DIST = 1024
T5_HEADS = N_DIL * A_HEADS + D_Q_HEADS
D_FF = 2816
CONV_W = 3

A_COLS = 3 * N_DIL * A_HEADS * A_HEAD_DIM
C_COLS = 3 * C_HEADS * C_HEAD_DIM
IN_SPLITS = (A_COLS, B_Q_RANK, B_KV_RANK, B_ROPE_DIM, C_COLS,
             D_Q_HEADS * D_HEAD_DIM, D_KV_HEADS * D_HEAD_DIM, D_KV_HEADS * D_HEAD_DIM)
IN_COLS = sum(IN_SPLITS)

kernel_name = "hybrid_gated_multi_mixer_encoder"


def rms_norm(x, g):
    xf = x.astype(jnp.float32)
    y = xf * lax.rsqrt(jnp.mean(xf * xf, axis=-1, keepdims=True) + RMS_EPS)
    return (y * g.astype(jnp.float32)).astype(x.dtype)


def split_columns(t):
    parts, start = [], 0
    for width in IN_SPLITS:
        parts.append(t[..., start:start + width])
        start += width
    return parts


def t5_bucket(rel):
    half = T5_BUCKETS // 2
    exact = half // 2
    n = np.abs(rel)
    large = exact + (np.log(np.maximum(n, 1) / exact) / math.log(T5_MAX_DIST / exact)
                     * (half - exact)).astype(np.int32)
    large = np.minimum(large, half - 1)
    return (np.where(rel > 0, half, 0) + np.where(n < exact, n, large)).astype(np.int32)


def relative_bias(t5_table, rel, h0, h1):
    return t5_table[t5_bucket(rel)][:, h0:h1].T


def band_attention(q, k, v, bias, radius, sink=None):
    N, L, Hq, dh = q.shape
    Hkv = k.shape[2]
    rep = Hq // Hkv
    blk = radius
    nblk = -(-L // blk)
    Lp = nblk * blk
    qb = jnp.pad(q, ((0, 0), (0, Lp - L), (0, 0), (0, 0))).reshape(N, nblk, blk, Hkv, rep, dh)

    def bands(t):
        tp = jnp.pad(t, ((0, 0), (blk, Lp - L + blk), (0, 0), (0, 0))).reshape(N, nblk + 2, blk, Hkv, dh)
        return jnp.concatenate([tp[:, :-2], tp[:, 1:-1], tp[:, 2:]], axis=2)

    kb, vb = bands(k), bands(v)
    s = jnp.einsum("nbqgrd,nbkgd->nbgrqk", qb, kb).astype(jnp.float32) * (dh ** -0.5)
    rel = np.arange(3 * blk)[None, :] - blk - np.arange(blk)[:, None]
    kpos = np.arange(nblk)[:, None] * blk - blk + np.arange(3 * blk)[None, :]
    mask = (np.abs(rel) <= radius)[None] & ((kpos >= 0) & (kpos < L))[:, None, :]
    b = bias[:, np.clip(rel + radius, 0, 2 * radius)].astype(jnp.float32).reshape(Hkv, rep, blk, 3 * blk)
    s = jnp.where(mask[None, :, None, None], s + b[None, None], NEG_INF)
    m = jnp.max(s, axis=-1)
    if sink is not None:
        sk = sink.astype(jnp.float32).reshape(Hkv, rep)[None, None, :, :, None]
        m = jnp.maximum(m, sk)
    p = jnp.exp(s - m[..., None])
    l = jnp.sum(p, axis=-1)
    if sink is not None:
        l = l + jnp.exp(sk - m)
    o = jnp.einsum("nbgrqk,nbkgd->nbqgrd", p.astype(v.dtype), vb).astype(jnp.float32)
    m_t = m.transpose(0, 1, 4, 2, 3)
    l_t = l.transpose(0, 1, 4, 2, 3)
    o = (o / l_t[..., None]).reshape(N, Lp, Hq, dh)[:, :L].astype(q.dtype)
    return o, m_t.reshape(N, Lp, Hq)[:, :L], l_t.reshape(N, Lp, Hq)[:, :L]


def to_strided(t, dil):
    B, S = t.shape[:2]
    return t.reshape(B, S // dil, dil, *t.shape[2:]).swapaxes(1, 2).reshape(B * dil, S // dil, *t.shape[2:])


def from_strided(t, dil):
    Bd, Sd = t.shape[:2]
    return t.reshape(Bd // dil, dil, Sd, *t.shape[2:]).swapaxes(1, 2).reshape(Bd // dil, Sd * dil, *t.shape[2:])


def dilated_attention(q, k, v, t5_table):
    outs, ms, ls = [], [], []
    for g, (window, dil) in enumerate(DIL_CFG):
        radius = window // dil // 2
        bias = relative_bias(t5_table, dil * np.arange(-radius, radius + 1), g * A_HEADS, (g + 1) * A_HEADS)
        o, m, l = band_attention(to_strided(q[:, :, g], dil), to_strided(k[:, :, g], dil),
                                 to_strided(v[:, :, g], dil), bias, radius)
        outs.append(from_strided(o, dil))
        ms.append(from_strided(m, dil))
        ls.append(from_strided(l, dil))
    m_all = jnp.stack(ms)
    wgt = jnp.stack(ls) * jnp.exp(m_all - jnp.max(m_all, axis=0, keepdims=True))
    o_all = jnp.stack(outs).astype(jnp.float32)
    out = jnp.sum(wgt[..., None] * o_all, axis=0) / jnp.sum(wgt, axis=0)[..., None]
    return out.astype(q.dtype)


def rope(x):
    S, half = x.shape[1], x.shape[-1] // 2
    inv = ROPE_THETA ** (-jnp.arange(half, dtype=jnp.float32) / half)
    ang = jnp.arange(S, dtype=jnp.float32)[:, None] * inv[None, :]
    shape = (1, S) + (1,) * (x.ndim - 3) + (half,)
    cos, sin = jnp.cos(ang).reshape(shape), jnp.sin(ang).reshape(shape)
    xf = x.astype(jnp.float32)
    x1, x2 = xf[..., :half], xf[..., half:]
    return jnp.concatenate([x1 * cos - x2 * sin, x1 * sin + x2 * cos], axis=-1).astype(x.dtype)


def dense_attention(q, k, v):
    B, S, H, dq = q.shape
    nb = S // Q_BLOCK
    qb = q.reshape(B, nb, Q_BLOCK, H, dq).transpose(1, 0, 2, 3, 4)

    def one_block(qi):
        s = jnp.einsum("bqhd,bkhd->bhqk", qi, k).astype(jnp.float32) * (dq ** -0.5)
        p = jax.nn.softmax(s, axis=-1)
        return jnp.einsum("bhqk,bkhd->bqhd", p.astype(v.dtype), v)

    o = lax.map(one_block, qb)
    return o.transpose(1, 0, 2, 3, 4).reshape(B, S, H, v.shape[-1])


def mla_attention(c_q, c_kv, k_rope, q_norm_g, w_uq, kv_norm_g, w_ukv):
    B, S = c_q.shape[:2]
    q = (rms_norm(c_q, q_norm_g) @ w_uq).reshape(B, S, B_HEADS, B_NOPE_DIM + B_ROPE_DIM)
    kv = (rms_norm(c_kv, kv_norm_g) @ w_ukv).reshape(B, S, B_HEADS, B_NOPE_DIM + B_V_DIM)
    q = jnp.concatenate([q[..., :B_NOPE_DIM], rope(q[..., B_NOPE_DIM:])], axis=-1)
    kr = jnp.broadcast_to(rope(k_rope)[:, :, None, :], (B, S, B_HEADS, B_ROPE_DIM))
    k = jnp.concatenate([kv[..., :B_NOPE_DIM], kr], axis=-1)
    return dense_attention(q, k, kv[..., B_NOPE_DIM:])


def neighbourhood_attention(q, k, v, rpb):
    B, S, H, dh = q.shape
    rows = S // GRID_W
    kh = min(NA_ROWS, rows)
    ncb = GRID_W // NA_COLS
    r = np.arange(rows)
    row_idx = np.clip(r - kh // 2, 0, rows - kh)[:, None] + np.arange(kh)[None, :]
    cb_start = np.clip(np.arange(ncb) * NA_COLS - NA_COLS // 2, 0, GRID_W - NA_KEY_COLS)
    col_idx = cb_start[:, None] + np.arange(NA_KEY_COLS)[None, :]
    qcol = np.arange(ncb)[:, None] * NA_COLS + np.arange(NA_COLS)[None, :]
    qcol_start = np.clip(qcol - NA_COLS // 2, 0, GRID_W - NA_COLS)
    kc = col_idx[:, None, :]
    col_ok = (kc >= qcol_start[..., None]) & (kc < qcol_start[..., None] + NA_COLS)
    dc_idx = np.clip(kc - qcol[..., None] + NA_COLS - 1, 0, 2 * NA_COLS - 2)
    dr_idx = row_idx - r[:, None] + NA_ROWS - 1

    gather = (slice(None), row_idx[:, :, None, None], col_idx[None, None])
    kg = k.reshape(B, rows, GRID_W, H, dh)[gather]
    vg = v.reshape(B, rows, GRID_W, H, dh)[gather]
    qg = q.reshape(B, rows, ncb, NA_COLS, H, dh)
    s = jnp.einsum("brcqhd,brkcwhd->brchqkw", qg, kg).astype(jnp.float32) * (dh ** -0.5)
    bias = rpb[:, dr_idx][..., dc_idx].astype(jnp.float32)
    s = s + bias.transpose(1, 3, 0, 4, 2, 5)[None]
    s = jnp.where(col_ok[None, None, :, None, :, None, :], s, NEG_INF)
    p = jax.nn.softmax(s.reshape(s.shape[:-2] + (kh * NA_KEY_COLS,)), axis=-1).reshape(s.shape)
    o = jnp.einsum("brchqkw,brkcwhd->brcqhd", p.astype(v.dtype), vg)
    return o.reshape(B, S, H, dh)


def dwconv_centered(u, w, b):
    F = u.shape[-1]
    y = lax.conv_general_dilated(u, w[:, None, :].astype(u.dtype), window_strides=(1,),
                                 padding=((CONV_W // 2, CONV_W // 2),),
                                 dimension_numbers=("NWC", "WIO", "NWC"), feature_group_count=F)
    return y + b.astype(u.dtype)


def setup_inputs(seed: int = 0) -> dict:
    key = jax.random.key(seed)
    ks = jax.random.split(key, 20)
    L, D, f32 = DEPTH, D_MODEL, jnp.float32

    def nrm(k, shape, scale):
        return jax.random.normal(k, shape, f32) * scale

    def gain(k, shape):
        return 1.0 + 0.02 * jax.random.normal(k, shape, f32)

    return {
        "x": nrm(ks[0], (BATCH, SEQ, D), 1.0),
        "t5_table": nrm(ks[1], (T5_BUCKETS, T5_HEADS), 0.3),
        "norm_mix_g": gain(ks[2], (L, D)),
        "w_in": nrm(ks[3], (L, D, IN_COLS), D ** -0.5),
        "q_norm_g": gain(ks[4], (L, B_Q_RANK)),
        "w_uq": nrm(ks[5], (L, B_Q_RANK, B_HEADS * (B_NOPE_DIM + B_ROPE_DIM)), B_Q_RANK ** -0.5),
        "kv_norm_g": gain(ks[6], (L, B_KV_RANK)),
        "w_ukv": nrm(ks[7], (L, B_KV_RANK, B_HEADS * (B_NOPE_DIM + B_V_DIM)), B_KV_RANK ** -0.5),
        "na_bias": nrm(ks[8], (L, C_HEADS, 2 * NA_ROWS - 1, 2 * NA_COLS - 1), 0.3),
        "sink_logit": nrm(ks[9], (L, D_Q_HEADS), 0.5),
        "w_gate": nrm(ks[10], (L, N_BRANCH, D, D), D ** -0.5),
        "w_branch": nrm(ks[11], (L, N_BRANCH, BRANCH_W, D), BRANCH_W ** -0.5),
        "w_out": nrm(ks[12], (L, D, D), D ** -0.5),
        "norm_ffn_g": gain(ks[13], (L, D)),
        "w_ffn_gate": nrm(ks[14], (L, D, D_FF), D ** -0.5),
        "w_ffn_up": nrm(ks[15], (L, D, D_FF), D ** -0.5),
        "conv_w": nrm(ks[16], (L, CONV_W, D_FF), CONV_W ** -0.5),
        "conv_b": nrm(ks[17], (L, D_FF), 0.02),
        "w_ffn_down": nrm(ks[18], (L, D_FF, D), D_FF ** -0.5),
        "final_g": gain(ks[19], (D,)),
    }


def reference(x, t5_table, norm_mix_g, w_in, q_norm_g, w_uq, kv_norm_g, w_ukv, na_bias, sink_logit,
              w_gate, w_branch, w_out, norm_ffn_g, w_ffn_gate, w_ffn_up, conv_w, conv_b, w_ffn_down,
              final_g):
    B, S, _ = x.shape
    d_bias = relative_bias(t5_table, np.arange(-D_RADIUS, D_RADIUS + 1), N_DIL * A_HEADS, T5_HEADS)
    for layer in range(DEPTH):
        h = rms_norm(x, norm_mix_g[layer])
        a_qkv, b_cq, b_ckv, b_kr, c_qkv, d_q, d_k, d_v = split_columns(h @ w_in[layer])

        a = a_qkv.reshape(B, S, 3, N_DIL, A_HEADS, A_HEAD_DIM)
        y_a = dilated_attention(a[:, :, 0], a[:, :, 1], a[:, :, 2], t5_table)

        y_b = mla_attention(b_cq, b_ckv, b_kr, q_norm_g[layer], w_uq[layer], kv_norm_g[layer], w_ukv[layer])

        c = c_qkv.reshape(B, S, 3, C_HEADS, C_HEAD_DIM)
        y_c = neighbourhood_attention(c[:, :, 0], c[:, :, 1], c[:, :, 2], na_bias[layer])

        y_d, _, _ = band_attention(d_q.reshape(B, S, D_Q_HEADS, D_HEAD_DIM),
                                   d_k.reshape(B, S, D_KV_HEADS, D_HEAD_DIM),
                                   d_v.reshape(B, S, D_KV_HEADS, D_HEAD_DIM),
                                   d_bias, D_RADIUS, sink_logit[layer])

        branches = (y_a, y_b, y_c, y_d)
        merged = jax.nn.sigmoid(h @ w_gate[layer, 0]) * (branches[0].reshape(B, S, BRANCH_W) @ w_branch[layer, 0])
        for i in range(1, N_BRANCH):
            merged = merged + jax.nn.sigmoid(h @ w_gate[layer, i]) * (
                branches[i].reshape(B, S, BRANCH_W) @ w_branch[layer, i])
        x = x + merged @ w_out[layer]

        h = rms_norm(x, norm_ffn_g[layer])
        u = jax.nn.gelu(dwconv_centered(h @ w_ffn_gate[layer], conv_w[layer], conv_b[layer])) * (h @ w_ffn_up[layer])
        x = x + u @ w_ffn_down[layer]
    return rms_norm(x, final_g)
```

```python
import functools
import math

import jax
import jax.numpy as jnp
import numpy as np
from jax import lax
from jax.experimental import pallas as pl
from jax.experimental.pallas import tpu as pltpu

D_MODEL = 1024
GRID_W = 64
RMS_EPS = 1e-6
NEG_INF = -1e30
N_BRANCH = 4
BRANCH_W = 256

DIL_CFG = ((128, 1), (512, 4), (2048, 16))
N_DIL = 3
A_HEADS = 4
HEAD_DIM = 64
B_HEADS = 4
B_Q_RANK = 256
B_KV_RANK = 128
B_NOPE_DIM = 64
B_ROPE_DIM = 32
B_V_DIM = 64
ROPE_THETA = 10000.0
C_HEADS = 4
NA_ROWS = 8
NA_COLS = 16
D_Q_HEADS = 4
D_KV_HEADS = 2
D_RADIUS = 128
T5_BUCKETS = 32
T5_MAX_DIST = 1024
T5_HEADS = N_DIL * A_HEADS + D_Q_HEADS
D_FF = 2816
CONV_W = 3

A_COLS = 3 * N_DIL * A_HEADS * HEAD_DIM
C_COLS = 3 * C_HEADS * HEAD_DIM

LANES = 128
SUBLANES = 8
VMEM_LIMIT_BYTES = 56 * 1024 * 1024

MAIN_A = 0
MAIN_C = A_COLS
MAIN_DQ = MAIN_C + C_COLS
MAIN_DK = MAIN_DQ + 256
MAIN_DV = MAIN_DK + 128
MAIN_COLS = MAIN_DV + 128
LAT_CQ = 0
LAT_CKV = 256
LAT_KR = 384
LAT_KRS = 512
LAT_COLS = 640

MLA_HEAD_PAD = 128

BF16 = jnp.bfloat16
F32 = jnp.float32


def _rms(x, g):
    return x * lax.rsqrt(jnp.mean(x * x, axis=-1, keepdims=True) + RMS_EPS) * g


def _dot(a, b):
    return jnp.dot(a, b, preferred_element_type=F32)


def _dot_nt(a, b):
    return lax.dot_general(a, b, (((1,), (1,)), ((), ())), preferred_element_type=F32)


def _params(*sem):
    return pltpu.CompilerParams(dimension_semantics=sem, vmem_limit_bytes=VMEM_LIMIT_BYTES)


def _resident(shape):
    nd = len(shape)
    return pl.BlockSpec(shape, lambda *_: (0,) * nd)


def _proj_in_kernel(x_ref, g_ref, w_ref, main_ref, lat_ref, *, chunk):
    h = _rms(x_ref[...], g_ref[...]).astype(BF16)
    for c0 in range(0, MAIN_COLS, chunk):
        main_ref[:, c0:c0 + chunk] = _dot(h, w_ref[:, c0:c0 + chunk]).astype(BF16)
    lat_ref[...] = _dot(h, w_ref[:, MAIN_COLS:])


def _proj_in(x2, g, w, tm):
    n = x2.shape[0]
    return pl.pallas_call(
        functools.partial(_proj_in_kernel, chunk=512),
        grid=(n // tm,),
        in_specs=[
            pl.BlockSpec((tm, D_MODEL), lambda i: (i, 0)),
            _resident((1, D_MODEL)),
            _resident((D_MODEL, MAIN_COLS + LAT_COLS)),
        ],
        out_specs=[
            pl.BlockSpec((tm, MAIN_COLS), lambda i: (i, 0)),
            pl.BlockSpec((tm, LAT_COLS), lambda i: (i, 0)),
        ],
        out_shape=[
            jax.ShapeDtypeStruct((n, MAIN_COLS), BF16),
            jax.ShapeDtypeStruct((n, LAT_COLS), F32),
        ],
        compiler_params=_params("parallel"),
        name="proj_in",
    )(x2, g, w)


def _band_kernel(*refs, seq, tq, radius, hq, hkv, has_sink, emit_stats):
    refs = list(refs)
    bias_ref = refs.pop(0)
    sink_ref = refs.pop(0) if has_sink else None
    q_ref, k_ref, v_ref, o_ref = refs[:4]
    st_ref = refs[4] if emit_stats else None
    win = tq + 2 * radius
    nblk = seq // tq
    rep = hq // hkv
    scale = HEAD_DIM ** -0.5

    def body(i, carry):
        q0 = pl.multiple_of(i * tq, tq)
        ks = pl.multiple_of(jnp.clip(q0 - radius, 0, seq - win), radius)
        variant = jnp.where(i == 0, 0, jnp.where(i == nblk - 1, 2, 1))
        qb = q_ref[0, pl.ds(q0, tq), :]
        kb = k_ref[0, pl.ds(ks, win), :]
        vb = v_ref[0, pl.ds(ks, win), :]
        outs = []
        if emit_stats:
            lane = lax.broadcasted_iota(jnp.int32, (tq, LANES), 1)
            stats = jnp.zeros((tq, LANES), F32)
        for h in range(hq):
            g = h // rep
            qh = qb[:, h * HEAD_DIM:(h + 1) * HEAD_DIM]
            kh = kb[:, g * HEAD_DIM:(g + 1) * HEAD_DIM]
            vh = vb[:, g * HEAD_DIM:(g + 1) * HEAD_DIM]
            s = _dot_nt(qh, kh) * scale + bias_ref[variant, h]
            m = jnp.max(s, axis=-1, keepdims=True)
            if has_sink:
                m = jnp.maximum(m, sink_ref[h])
            p = jnp.exp(s - m)
            l = jnp.sum(p, axis=-1, keepdims=True)
            if has_sink:
                l = l + jnp.exp(sink_ref[h] - m)
            pv = _dot(p.astype(BF16), vh)
            outs.append(pv / l)
            if emit_stats:
                stats = jnp.where(lane == h, m, stats)
                stats = jnp.where(lane == hq + h, l, stats)
        o_ref[0, pl.ds(q0, tq), :] = jnp.concatenate(outs, axis=-1).astype(o_ref.dtype)
        if emit_stats:
            st_ref[0, pl.ds(q0, tq), :] = stats
        return carry

    lax.fori_loop(0, nblk, body, 0)


def _band_attention(bias, sink, src, *, nstream, seq, cols, qcol, kcol, vcol, kvw, radius, hq, hkv,
                    out_dtype, emit_stats, name):
    nb = src.shape[0]
    tq = 128
    qw = hq * HEAD_DIM
    assert seq % tq == 0 and seq >= tq + 2 * radius and tq >= radius
    kernel = functools.partial(_band_kernel, seq=seq, tq=tq, radius=radius, hq=hq, hkv=hkv,
                               has_sink=sink is not None, emit_stats=emit_stats)
    in_specs = [_resident(bias.shape)]
    args = [bias]
    if sink is not None:
        in_specs.append(pl.BlockSpec(memory_space=pltpu.SMEM))
        args.append(sink)
    in_specs += [
        pl.BlockSpec((1, seq, qw), lambda b, r: (b, 0, r * (cols // qw) + qcol)),
        pl.BlockSpec((1, seq, kvw), lambda b, r: (b, 0, r * (cols // kvw) + kcol)),
        pl.BlockSpec((1, seq, kvw), lambda b, r: (b, 0, r * (cols // kvw) + vcol)),
    ]
    args += [src, src, src]
    out_specs = [pl.BlockSpec((1, seq, qw), lambda b, r: (b, 0, r))]
    out_shape = [jax.ShapeDtypeStruct((nb, seq, nstream * qw), out_dtype)]
    if emit_stats:
        out_specs.append(pl.BlockSpec((1, seq, LANES), lambda b, r: (b, 0, r)))
        out_shape.append(jax.ShapeDtypeStruct((nb, seq, nstream * LANES), F32))
    outs = pl.pallas_call(
        kernel,
        grid=(nb, nstream),
        in_specs=in_specs,
        out_specs=out_specs,
        out_shape=out_shape,
        compiler_params=_params("parallel", "parallel"),
        name=name,
    )(*args)
    return outs if emit_stats else outs[0]


def _t5_bucket(rel):
    half = T5_BUCKETS // 2
    exact = half // 2
    n = np.abs(rel)
    large = exact + (np.log(np.maximum(n, 1) / exact) / math.log(T5_MAX_DIST / exact)
                     * (half - exact)).astype(np.int32)
    large = np.minimum(large, half - 1)
    return (np.where(rel > 0, half, 0) + np.where(n < exact, n, large)).astype(np.int32)


def _band_bias(t5_table, h0, h1, radius, dil, tq=128):
    win = tq + 2 * radius
    offs = np.array([0, -radius, -2 * radius])
    rel = offs[:, None, None] + np.arange(win)[None, None, :] - np.arange(tq)[None, :, None]
    valid = np.abs(rel) <= radius
    bucket = _t5_bucket(dil * np.clip(rel, -radius, radius))
    b = t5_table[:, h0:h1][bucket]
    b = jnp.transpose(b, (0, 3, 1, 2)).astype(F32)
    return jnp.where(valid[:, None], b, NEG_INF)


def _mla_prep_kernel(cq_ref, ckv_ref, kr_ref, krs_ref, ct_ref, st_ref, gq_ref, gkv_ref,
                     wq1_ref, wq2_ref, wk_ref, wv_ref, q_ref, k_ref, v_ref):
    ct = jnp.concatenate([ct_ref[...]] * B_HEADS, axis=-1)
    st = jnp.concatenate([st_ref[...]] * B_HEADS, axis=-1)
    cn = _rms(cq_ref[...], gq_ref[...]).astype(BF16)
    q = _dot(cn, wq1_ref[...]) * ct + _dot(cn, wq2_ref[...]) * st
    q_ref[...] = q.astype(BF16)
    kvn = _rms(ckv_ref[...], gkv_ref[...]).astype(BF16)
    krope = kr_ref[...] * ct_ref[...] + krs_ref[...] * st_ref[...]
    k = _dot(kvn, wk_ref[...]) + jnp.concatenate([krope] * B_HEADS, axis=-1)
    k_ref[...] = k.astype(BF16)
    v_ref[...] = _dot(kvn, wv_ref[...]).astype(BF16)


def _mla_prep(lat, ct, st, gq, gkv, wq1, wq2, wk, wv, seq, tm):
    n = lat.shape[0]
    hw = B_HEADS * MLA_HEAD_PAD
    tps = seq // tm
    tab = pl.BlockSpec((tm, LANES), lambda i: (i % tps, 0))
    return pl.pallas_call(
        _mla_prep_kernel,
        grid=(n // tm,),
        in_specs=[
            pl.BlockSpec((tm, B_Q_RANK), lambda i: (i, LAT_CQ // B_Q_RANK)),
            pl.BlockSpec((tm, LANES), lambda i: (i, LAT_CKV // LANES)),
            pl.BlockSpec((tm, LANES), lambda i: (i, LAT_KR // LANES)),
            pl.BlockSpec((tm, LANES), lambda i: (i, LAT_KRS // LANES)),
            tab, tab,
            _resident(gq.shape), _resident(gkv.shape),
            _resident(wq1.shape), _resident(wq2.shape), _resident(wk.shape), _resident(wv.shape),
        ],
        out_specs=[pl.BlockSpec((tm, hw), lambda i: (i, 0))] * 3,
        out_shape=[jax.ShapeDtypeStruct((n, hw), BF16)] * 3,
        compiler_params=_params("parallel"),
        name="mla_prep",
    )(lat, lat, lat, lat, ct, st, gq, gkv, wq1, wq2, wk, wv)


def _mla_attn_kernel(q_ref, k_ref, v_ref, o_ref, *, seq, tq, tk):
    scale = (B_NOPE_DIM + B_ROPE_DIM) ** -0.5
    outs = []
    for h in range(B_HEADS):
        hs = slice(h * MLA_HEAD_PAD, (h + 1) * MLA_HEAD_PAD)
        qh = q_ref[0, :, hs]

        def body(c, carry):
            m, l, acc = carry
            k0 = pl.multiple_of(c * tk, tk)
            kc = k_ref[0, pl.ds(k0, tk), hs]
            vc = v_ref[0, pl.ds(k0, tk), hs]
            s = _dot_nt(qh, kc) * scale
            m_new = jnp.maximum(m, jnp.max(s, axis=-1, keepdims=True))
            alpha = jnp.exp(m - m_new)
            p = jnp.exp(s - m_new)
            l = alpha * l + jnp.sum(p, axis=-1, keepdims=True)
            acc = alpha * acc + _dot(p.astype(BF16), vc)
            return m_new, l, acc

        init = (jnp.full((tq, 1), -jnp.inf, F32), jnp.zeros((tq, 1), F32), jnp.zeros((tq, MLA_HEAD_PAD), F32))
        _, l, acc = lax.fori_loop(0, seq // tk, body, init)
        outs.append((acc / l)[:, :B_V_DIM])
    o_ref[0] = jnp.concatenate(outs, axis=-1).astype(o_ref.dtype)


def _mla_attn(q, k, v, tq, tk):
    nb, seq, hw = q.shape
    return pl.pallas_call(
        functools.partial(_mla_attn_kernel, seq=seq, tq=tq, tk=tk),
        grid=(nb, seq // tq),
        in_specs=[
            pl.BlockSpec((1, tq, hw), lambda b, i: (b, i, 0)),
            pl.BlockSpec((1, seq, hw), lambda b, i: (b, 0, 0)),
            pl.BlockSpec((1, seq, hw), lambda b, i: (b, 0, 0)),
        ],
        out_specs=pl.BlockSpec((1, tq, BRANCH_W), lambda b, i: (b, i, 0)),
        out_shape=jax.ShapeDtypeStruct((nb, seq, BRANCH_W), BF16),
        compiler_params=_params("parallel", "parallel"),
        name="mla_attn",
    )(q, k, v)


def _rope_tables(seq):
    half = B_ROPE_DIM // 2
    inv = ROPE_THETA ** (-jnp.arange(half, dtype=F32) / half)
    ang = jnp.arange(seq, dtype=F32)[:, None] * inv[None, :]
    cos, sin = jnp.cos(ang), jnp.sin(ang)
    pad = jnp.zeros((seq, MLA_HEAD_PAD - B_NOPE_DIM - B_ROPE_DIM), F32)
    ct = jnp.concatenate([jnp.ones((seq, B_NOPE_DIM), F32), cos, cos, pad], axis=-1)
    st = jnp.concatenate([jnp.zeros((seq, B_NOPE_DIM), F32), -sin, sin, pad], axis=-1)
    return ct, st


def _mla_weights(w_uq, w_ukv):
    half = B_ROPE_DIM // 2
    wq = w_uq.reshape(B_Q_RANK, B_HEADS, B_NOPE_DIM + B_ROPE_DIM)
    nope, r1, r2 = wq[..., :B_NOPE_DIM], wq[..., B_NOPE_DIM:B_NOPE_DIM + half], wq[..., B_NOPE_DIM + half:]
    zpad = jnp.zeros((B_Q_RANK, B_HEADS, MLA_HEAD_PAD - B_NOPE_DIM - B_ROPE_DIM), F32)
    wq1 = jnp.concatenate([nope, r1, r2, zpad], axis=-1).reshape(B_Q_RANK, -1)
    wq2 = jnp.concatenate([jnp.zeros_like(nope), r2, r1, zpad], axis=-1).reshape(B_Q_RANK, -1)
    wkv = w_ukv.reshape(B_KV_RANK, B_HEADS, B_NOPE_DIM + B_V_DIM)
    z64 = jnp.zeros((B_KV_RANK, B_HEADS, MLA_HEAD_PAD - B_NOPE_DIM), F32)
    wk = jnp.concatenate([wkv[..., :B_NOPE_DIM], z64], axis=-1).reshape(B_KV_RANK, -1)
    wv = jnp.concatenate([wkv[..., B_NOPE_DIM:], z64], axis=-1).reshape(B_KV_RANK, -1)
    return tuple(w.astype(BF16) for w in (wq1, wq2, wk, wv))


def _na_kernel(bias_ref, q_ref, k_ref, v_ref, o_ref, *, rows):
    scale = HEAD_DIM ** -0.5
    kh = NA_ROWS
    win = kh * GRID_W

    def body(r, carry):
        q0 = pl.multiple_of(r * GRID_W, GRID_W)
        r0 = jnp.clip(r - kh // 2, 0, rows - kh)
        ks = pl.multiple_of(r0 * GRID_W, GRID_W)
        variant = r - r0
        qb = q_ref[0, pl.ds(q0, GRID_W), :]
        kb = k_ref[0, pl.ds(ks, win), :]
        vb = v_ref[0, pl.ds(ks, win), :]
        outs = []
        for h in range(C_HEADS):
            hs = slice(h * HEAD_DIM, (h + 1) * HEAD_DIM)
            s = _dot_nt(qb[:, hs], kb[:, hs]) * scale + bias_ref[variant, h]
            m = jnp.max(s, axis=-1, keepdims=True)
            p = jnp.exp(s - m)
            l = jnp.sum(p, axis=-1, keepdims=True)
            outs.append(_dot(p.astype(BF16), vb[:, hs]) / l)
        o_ref[0, pl.ds(q0, GRID_W), :] = jnp.concatenate(outs, axis=-1).astype(o_ref.dtype)
        return carry

    lax.fori_loop(0, rows, body, 0)


def _na_attention(bias, src, seq, qcol):
    nb = src.shape[0]
    rows = seq // GRID_W
    assert rows >= NA_ROWS
    spec = lambda c: pl.BlockSpec((1, seq, BRANCH_W), lambda b: (b, 0, c))
    return pl.pallas_call(
        functools.partial(_na_kernel, rows=rows),
        grid=(nb,),
        in_specs=[_resident(bias.shape), spec(qcol), spec(qcol + 1), spec(qcol + 2)],
        out_specs=pl.BlockSpec((1, seq, BRANCH_W), lambda b: (b, 0, 0)),
        out_shape=jax.ShapeDtypeStruct((nb, seq, BRANCH_W), BF16),
        compiler_params=_params("parallel"),
        name="na_attn",
    )(bias, src, src, src)


def _na_bias(rpb):
    kh = NA_ROWS
    qc = np.arange(GRID_W)[:, None]
    kc = np.arange(GRID_W)[None, :]
    start = np.clip(qc - NA_COLS // 2, 0, GRID_W - NA_COLS)
    col_ok = (kc >= start) & (kc < start + NA_COLS)
    dc = np.clip(kc - qc + NA_COLS - 1, 0, 2 * NA_COLS - 2)
    v = np.arange(kh)[:, None]
    dr = np.clip(np.arange(kh)[None, :] - v + NA_ROWS - 1, 0, 2 * NA_ROWS - 2)
    b = rpb[:, dr][..., dc]
    b = jnp.where(col_ok[None, None, None], b.astype(F32), NEG_INF)
    b = jnp.transpose(b, (1, 0, 3, 2, 4))
    return b.reshape(kh, rpb.shape[0], GRID_W, kh * GRID_W)


def _merge_kernel(x_ref, g_ref, oa0_ref, oa1_ref, oa2_ref, sa0_ref, sa1_ref, sa2_ref, yb_ref, yc_ref, yd_ref,
                  wg_ref, wb_ref, wo_ref, out_ref):
    x = x_ref[...]
    h = _rms(x, g_ref[...]).astype(BF16)

    o = [oa0_ref[...], oa1_ref[...], oa2_ref[...]]
    st = [sa0_ref[...], sa1_ref[...], sa2_ref[...]]
    m = [s[:, 0:A_HEADS] for s in st]
    l = [s[:, A_HEADS:2 * A_HEADS] for s in st]
    m_all = jnp.maximum(jnp.maximum(m[0], m[1]), m[2])
    wgt = [l[g] * jnp.exp(m[g] - m_all) for g in range(N_DIL)]
    den = wgt[0] + wgt[1] + wgt[2]
    parts = []
    for hh in range(A_HEADS):
        hs = slice(hh * HEAD_DIM, (hh + 1) * HEAD_DIM)
        num = wgt[0][:, hh:hh + 1] * o[0][:, hs]
        for g in range(1, N_DIL):
            num = num + wgt[g][:, hh:hh + 1] * o[g][:, hs]
        parts.append(num / den[:, hh:hh + 1])
    ya = jnp.concatenate(parts, axis=-1).astype(BF16)

    branches = [ya, yb_ref[...], yc_ref[...], yd_ref[...]]
    merged = None
    for i in range(N_BRANCH):
        gate = jax.nn.sigmoid(_dot(h, wg_ref[i]))
        term = gate * _dot(branches[i], wb_ref[i])
        merged = term if merged is None else merged + term
    out_ref[...] = x + _dot(merged.astype(BF16), wo_ref[...])


def _merge(x2, g, oa, sa, yb, yc, yd, wg, wb, wo, tm):
    n = x2.shape[0]
    row = lambda w: pl.BlockSpec((tm, w), lambda i: (i, 0))
    return pl.pallas_call(
        _merge_kernel,
        grid=(n // tm,),
        in_specs=[row(D_MODEL), _resident(g.shape)] + [row(BRANCH_W)] * 3 + [row(LANES)] * 3
                 + [row(BRANCH_W)] * 3 + [_resident(wg.shape), _resident(wb.shape), _resident(wo.shape)],
        out_specs=row(D_MODEL),
        out_shape=jax.ShapeDtypeStruct((n, D_MODEL), F32),
        compiler_params=_params("parallel"),
        name="merge",
    )(x2, g, *oa, *sa, yb, yc, yd, wg, wb, wo)


def _gelu_tanh(x):
    return x * (0.5 * (1.0 + jnp.tanh(math.sqrt(2.0 / math.pi) * (x + 0.044715 * (x * x * x)))))


def _ffn_kernel(*refs, tm, tiles_per_seq, nchunk, final):
    refs = list(refs)
    xp_ref, x_ref, xn_ref, g_ref, wg_ref, wu_ref, cw_ref, cb_ref, wd_ref = refs[:9]
    fg_ref = refs[9] if final else None
    out_ref, acc_ref = refs[-2:]
    i = pl.program_id(0)
    t = i % tiles_per_seq
    keep_prev = (t != 0).astype(F32)
    keep_next = (t != tiles_per_seq - 1).astype(F32)
    x = x_ref[...]
    g = g_ref[...]
    h = _rms(x, g).astype(BF16)
    hp = _rms(xp_ref[...], g).astype(BF16)
    hn = _rms(xn_ref[...], g).astype(BF16)
    row = lax.broadcasted_iota(jnp.int32, (tm, 1), 0)
    acc_ref[...] = jnp.zeros_like(acc_ref)

    def body(c, carry):
        wg = wg_ref[c]
        gate = _dot(h, wg)
        gp = _dot(hp, wg)[SUBLANES - 1:SUBLANES, :] * keep_prev
        gn = _dot(hn, wg)[0:1, :] * keep_next
        prev = jnp.where(row == 0, gp, pltpu.roll(gate, 1, axis=0))
        nxt = jnp.where(row == tm - 1, gn, pltpu.roll(gate, tm - 1, axis=0))
        cw = cw_ref[c]
        conv = cw[0:1, :] * prev + cw[1:2, :] * gate + cw[2:3, :] * nxt + cb_ref[c]
        u = _gelu_tanh(conv) * _dot(h, wu_ref[c])
        acc_ref[...] += _dot(u.astype(BF16), wd_ref[c])
        return carry

    lax.fori_loop(0, nchunk, body, 0)
    y = x + acc_ref[...]
    if final:
        y = _rms(y, fg_ref[...])
    out_ref[...] = y


def _ffn(x2, g, wg, wu, cw, cb, wd, final_g, seq, tm):
    n = x2.shape[0]
    nchunk = wg.shape[0]
    nb8 = n // SUBLANES
    step = tm // SUBLANES
    final = final_g is not None
    in_specs = [
        pl.BlockSpec((SUBLANES, D_MODEL), lambda i: (jnp.maximum(i * step - 1, 0), 0)),
        pl.BlockSpec((tm, D_MODEL), lambda i: (i, 0)),
        pl.BlockSpec((SUBLANES, D_MODEL), lambda i: (jnp.minimum((i + 1) * step, nb8 - 1), 0)),
        _resident(g.shape), _resident(wg.shape), _resident(wu.shape), _resident(cw.shape),
        _resident(cb.shape), _resident(wd.shape),
    ]
    args = [x2, x2, x2, g, wg, wu, cw, cb, wd]
    if final:
        in_specs.append(_resident(final_g.shape))
        args.append(final_g)
    return pl.pallas_call(
        functools.partial(_ffn_kernel, tm=tm, tiles_per_seq=seq // tm, nchunk=nchunk, final=final),
        grid=(n // tm,),
        in_specs=in_specs,
        out_specs=pl.BlockSpec((tm, D_MODEL), lambda i: (i, 0)),
        out_shape=jax.ShapeDtypeStruct((n, D_MODEL), F32),
        scratch_shapes=[pltpu.VMEM((tm, D_MODEL), F32)],
        compiler_params=_params("parallel"),
        name="ffn",
    )(*args)


def _proj_in_weight(w):
    a, cq, ckv, kr, c, dq, dk, dv = jnp.split(w, np.cumsum(
        [A_COLS, B_Q_RANK, B_KV_RANK, B_ROPE_DIM, C_COLS, D_Q_HEADS * HEAD_DIM, D_KV_HEADS * HEAD_DIM])[:], axis=1)
    half = B_ROPE_DIM // 2
    z = lambda width: jnp.zeros((D_MODEL, width), w.dtype)
    kr_blk = jnp.concatenate([z(B_NOPE_DIM), kr, z(LANES - B_NOPE_DIM - B_ROPE_DIM)], axis=1)
    krs_blk = jnp.concatenate([z(B_NOPE_DIM), kr[:, half:], kr[:, :half], z(LANES - B_NOPE_DIM - B_ROPE_DIM)], axis=1)
    return jnp.concatenate([a, c, dq, dk, dv, cq, ckv, kr_blk, krs_blk], axis=1).astype(BF16)


def _ffn_chunks(w, axis, fc):
    if axis == 1:
        return jnp.transpose(w.reshape(w.shape[0], D_FF // fc, fc), (1, 0, 2))
    return w.reshape(D_FF // fc, fc, w.shape[1])


def _layer(x2, nb, seq, t5_table, d_bias, ct, st, p, final_g):
    n = x2.shape[0]
    tm = 512
    main, lat = _proj_in(x2, p["norm_mix_g"], p["w_in"], tm)

    oa, sa = [], []
    for gi, (window, dil) in enumerate(DIL_CFG):
        radius = window // dil // 2
        bias = _band_bias(t5_table, gi * A_HEADS, (gi + 1) * A_HEADS, radius, dil)
        src = main.reshape(nb, seq // dil, dil * MAIN_COLS)
        o, s = _band_attention(bias, None, src, nstream=dil, seq=seq // dil, cols=MAIN_COLS,
                               qcol=gi, kcol=N_DIL + gi, vcol=2 * N_DIL + gi, kvw=BRANCH_W, radius=radius,
                               hq=A_HEADS, hkv=A_HEADS, out_dtype=F32, emit_stats=True, name=f"band_a{gi}")
        oa.append(o.reshape(n, BRANCH_W))
        sa.append(s.reshape(n, LANES))

    q, k, v = _mla_prep(lat, ct, st, p["q_norm_g"], p["kv_norm_g"], *p["mla_w"], seq, tm)
    hw = B_HEADS * MLA_HEAD_PAD
    yb = _mla_attn(q.reshape(nb, seq, hw), k.reshape(nb, seq, hw), v.reshape(nb, seq, hw), 512, 512)

    src = main.reshape(nb, seq, MAIN_COLS)
    yc = _na_attention(_na_bias(p["na_bias"]), src, seq, MAIN_C // BRANCH_W)

    yd = _band_attention(d_bias, p["sink_logit"], src, nstream=1, seq=seq, cols=MAIN_COLS,
                         qcol=MAIN_DQ // BRANCH_W, kcol=MAIN_DK // LANES, vcol=MAIN_DV // LANES, kvw=LANES,
                         radius=D_RADIUS, hq=D_Q_HEADS, hkv=D_KV_HEADS, out_dtype=BF16, emit_stats=False,
                         name="band_d")

    x2 = _merge(x2, p["norm_mix_g"], oa, sa, yb.reshape(n, BRANCH_W), yc.reshape(n, BRANCH_W),
                yd.reshape(n, BRANCH_W), p["w_gate"], p["w_branch"], p["w_out"], tm)
    return _ffn(x2, p["norm_ffn_g"], p["w_ffn_gate"], p["w_ffn_up"], p["conv_w"], p["conv_b"],
                p["w_ffn_down"], final_g, seq, tm)


def kernel(x, t5_table, norm_mix_g, w_in, q_norm_g, w_uq, kv_norm_g, w_ukv, na_bias, sink_logit, w_gate, w_branch, w_out, norm_ffn_g, w_ffn_gate, w_ffn_up, conv_w, conv_b, w_ffn_down, final_g):
    nb, seq, _ = x.shape
    depth = w_in.shape[0]
    fc = 256
    x2 = x.reshape(nb * seq, D_MODEL)
    ct, st = _rope_tables(seq)
    d_bias = _band_bias(t5_table, N_DIL * A_HEADS, T5_HEADS, D_RADIUS, 1)
    for layer in range(depth):
        p = {
            "norm_mix_g": norm_mix_g[layer][None, :],
            "w_in": _proj_in_weight(w_in[layer]),
            "q_norm_g": q_norm_g[layer][None, :],
            "kv_norm_g": kv_norm_g[layer][None, :],
            "mla_w": _mla_weights(w_uq[layer], w_ukv[layer]),
            "na_bias": na_bias[layer],
            "sink_logit": sink_logit[layer],
            "w_gate": w_gate[layer].astype(BF16),
            "w_branch": w_branch[layer].astype(BF16),
            "w_out": w_out[layer].astype(BF16),
            "norm_ffn_g": norm_ffn_g[layer][None, :],
            "w_ffn_gate": _ffn_chunks(w_ffn_gate[layer], 1, fc).astype(BF16),
            "w_ffn_up": _ffn_chunks(w_ffn_up[layer], 1, fc).astype(BF16),
            "conv_w": _ffn_chunks(conv_w[layer], 1, fc),
            "conv_b": conv_b[layer].reshape(D_FF // fc, 1, fc),
            "w_ffn_down": _ffn_chunks(w_ffn_down[layer], 0, fc).astype(BF16),
        }
        x2 = _layer(x2, nb, seq, t5_table, d_bias, ct, st, p,
                    final_g[None, :] if layer == depth - 1 else None)
    return x2.reshape(nb, seq, D_MODEL)
```

```python
import functools
import math

import jax
import jax.numpy as jnp
import numpy as np
from jax import lax
from jax.experimental import pallas as pl
from jax.experimental.pallas import tpu as pltpu

D_MODEL = 1024
GRID_W = 64
RMS_EPS = 1e-6
NEG_INF = -1e30
N_BRANCH = 4
BRANCH_W = 256

DIL_CFG = ((128, 1), (512, 4), (2048, 16))
N_DIL = 3
A_HEADS = 4
HEAD_DIM = 64
B_HEADS = 4
B_Q_RANK = 256
B_KV_RANK = 128
B_NOPE_DIM = 64
B_ROPE_DIM = 32
B_V_DIM = 64
ROPE_THETA = 10000.0
C_HEADS = 4
NA_ROWS = 8
NA_COLS = 16
D_Q_HEADS = 4
D_KV_HEADS = 2
D_RADIUS = 128
T5_BUCKETS = 32
T5_MAX_DIST = 1024
T5_HEADS = N_DIL * A_HEADS + D_Q_HEADS
D_FF = 2816
CONV_W = 3

A_COLS = 3 * N_DIL * A_HEADS * HEAD_DIM
C_COLS = 3 * C_HEADS * HEAD_DIM

LANES = 128
SUBLANES = 8
VMEM_LIMIT_BYTES = 56 * 1024 * 1024

MAIN_A = 0
MAIN_C = A_COLS
MAIN_DQ = MAIN_C + C_COLS
MAIN_DK = MAIN_DQ + 256
MAIN_DV = MAIN_DK + 128
MAIN_COLS = MAIN_DV + 128
LAT_CQ = 0
LAT_CKV = 256
LAT_KR = 384
LAT_KRS = 512
LAT_COLS = 640

MLA_HEAD_PAD = 128

BF16 = jnp.bfloat16
F32 = jnp.float32
LOG2E = math.log2(math.e)
QK_SCALE = HEAD_DIM ** -0.5
assert math.frexp(QK_SCALE)[0] == 0.5


def _rms(x, g):
    return x * lax.rsqrt(jnp.mean(x * x, axis=-1, keepdims=True) + RMS_EPS) * g


def _dot(a, b):
    return jnp.dot(a, b, preferred_element_type=F32)


def _dot_nt(a, b):
    return lax.dot_general(a, b, (((1,), (1,)), ((), ())), preferred_element_type=F32)


def _params(*sem):
    return pltpu.CompilerParams(dimension_semantics=sem, vmem_limit_bytes=VMEM_LIMIT_BYTES)


def _resident(shape):
    nd = len(shape)
    return pl.BlockSpec(shape, lambda *_: (0,) * nd)


def _proj_in_kernel(x_ref, g_ref, w_ref, main_ref, lat_ref, *, chunk):
    h = _rms(x_ref[...], g_ref[...]).astype(BF16)
    for c0 in range(0, MAIN_COLS, chunk):
        main_ref[:, c0:c0 + chunk] = _dot(h, w_ref[:, c0:c0 + chunk]).astype(BF16)
    lat_ref[...] = _dot(h, w_ref[:, MAIN_COLS:])


def _proj_in(x2, g, w, tm):
    n = x2.shape[0]
    return pl.pallas_call(
        functools.partial(_proj_in_kernel, chunk=512),
        grid=(n // tm,),
        in_specs=[
            pl.BlockSpec((tm, D_MODEL), lambda i: (i, 0)),
            _resident((1, D_MODEL)),
            _resident((D_MODEL, MAIN_COLS + LAT_COLS)),
        ],
        out_specs=[
            pl.BlockSpec((tm, MAIN_COLS), lambda i: (i, 0)),
            pl.BlockSpec((tm, LAT_COLS), lambda i: (i, 0)),
        ],
        out_shape=[
            jax.ShapeDtypeStruct((n, MAIN_COLS), BF16),
            jax.ShapeDtypeStruct((n, LAT_COLS), F32),
        ],
        compiler_params=_params("parallel"),
        name="proj_in",
    )(x2, g, w)


def _pair_rows(qv, lo):
    zero = jnp.zeros_like(qv)
    return jnp.concatenate([jnp.where(lo, qv, zero), jnp.where(lo, zero, qv)], axis=0)


def _pair_softmax(s, sink):
    m = jnp.max(s, axis=-1, keepdims=True)
    if sink is not None:
        m = jnp.maximum(m, sink)
    p = jnp.exp2(s - m)
    l = jnp.sum(p, axis=-1, keepdims=True)
    if sink is not None:
        l = l + jnp.exp2(sink - m)
    return p, m, l


def _band_kernel(*refs, seq, tq, radius, npair, nkv, has_sink, emit_stats):
    refs = list(refs)
    bias_ref = refs.pop(0)
    sink_ref = refs.pop(0) if has_sink else None
    q_ref, k_ref, v_ref, o_ref = refs[:4]
    st_ref = refs[4] if emit_stats else None
    win = tq + 2 * radius
    nblk = seq // tq
    lane = lax.broadcasted_iota(jnp.int32, (tq, LANES), 1)
    lo = lane < HEAD_DIM
    top = lax.broadcasted_iota(jnp.int32, (2 * tq, 1), 0) < tq

    ub = min(nblk, 4)
    assert nblk % ub == 0
    units = [(u, j) for u in range(ub) for j in range(npair)]
    kv_lanes = [slice((j if nkv == npair else 0) * LANES, ((j if nkv == npair else 0) + 1) * LANES)
                for j in range(npair)]

    def body(step, carry):
        q0 = [pl.multiple_of((step * ub + u) * tq, tq) for u in range(ub)]
        ks = [pl.multiple_of(jnp.clip(q - radius, 0, seq - win), radius) for q in q0]
        variant = [jnp.where(step * ub + u == 0, 0, jnp.where(step * ub + u == nblk - 1, 2, 1)) for u in range(ub)]
        s = {}
        for u, j in units:
            q2 = _pair_rows(q_ref[0, pl.ds(q0[u], tq), j * LANES:(j + 1) * LANES], lo)
            s[u, j] = _dot_nt(q2, k_ref[0, pl.ds(ks[u], win), kv_lanes[j]]) * LOG2E + bias_ref[variant[u], j]
        soft = {}
        for u, j in units:
            sink = jnp.where(top, sink_ref[2 * j], sink_ref[2 * j + 1]) if has_sink else None
            soft[u, j] = _pair_softmax(s[u, j], sink)
        for u in range(ub):
            outs = []
            stats = jnp.zeros((tq, LANES), F32)
            for j in range(npair):
                p, m, l = soft[u, j]
                pv = _dot(p.astype(BF16), v_ref[0, pl.ds(ks[u], win), kv_lanes[j]]) / l
                outs.append(jnp.where(lo, pv[:tq], pv[tq:]))
                if emit_stats:
                    for half, rows in enumerate((slice(0, tq), slice(tq, 2 * tq))):
                        stats = jnp.where(lane == 2 * j + half, m[rows], stats)
                        stats = jnp.where(lane == 2 * npair + 2 * j + half, l[rows], stats)
            o_ref[0, pl.ds(q0[u], tq), :] = jnp.concatenate(outs, axis=-1).astype(o_ref.dtype)
            if emit_stats:
                st_ref[0, pl.ds(q0[u], tq), :] = stats
        return carry

    lax.fori_loop(0, nblk // ub, body, 0)


def _band_attention(bias, sink, src, *, nstream, seq, cols, qcol, kcol, vcol, kvw, radius,
                    out_dtype, emit_stats, name):
    nb = src.shape[0]
    npair = bias.shape[1]
    tq = bias.shape[2] // 2
    qw = npair * LANES
    assert seq % tq == 0 and seq >= tq + 2 * radius and tq >= radius and bias.shape[3] == tq + 2 * radius
    kernel = functools.partial(_band_kernel, seq=seq, tq=tq, radius=radius, npair=npair, nkv=kvw // LANES,
                               has_sink=sink is not None, emit_stats=emit_stats)
    in_specs = [_resident(bias.shape)]
    args = [bias]
    if sink is not None:
        in_specs.append(pl.BlockSpec(memory_space=pltpu.SMEM))
        args.append(sink)
    in_specs += [
        pl.BlockSpec((1, seq, qw), lambda b, r: (b, 0, r * (cols // qw) + qcol)),
        pl.BlockSpec((1, seq, kvw), lambda b, r: (b, 0, r * (cols // kvw) + kcol)),
        pl.BlockSpec((1, seq, kvw), lambda b, r: (b, 0, r * (cols // kvw) + vcol)),
    ]
    args += [src, src, src]
    out_specs = [pl.BlockSpec((1, seq, qw), lambda b, r: (b, 0, r))]
    out_shape = [jax.ShapeDtypeStruct((nb, seq, nstream * qw), out_dtype)]
    if emit_stats:
        out_specs.append(pl.BlockSpec((1, seq, LANES), lambda b, r: (b, 0, r)))
        out_shape.append(jax.ShapeDtypeStruct((nb, seq, nstream * LANES), F32))
    outs = pl.pallas_call(
        kernel,
        grid=(nb, nstream),
        in_specs=in_specs,
        out_specs=out_specs,
        out_shape=out_shape,
        compiler_params=_params("parallel", "parallel"),
        name=name,
    )(*args)
    return outs if emit_stats else outs[0]


def _t5_bucket(rel):
    half = T5_BUCKETS // 2
    exact = half // 2
    n = np.abs(rel)
    large = exact + (np.log(np.maximum(n, 1) / exact) / math.log(T5_MAX_DIST / exact)
                     * (half - exact)).astype(np.int32)
    large = np.minimum(large, half - 1)
    return (np.where(rel > 0, half, 0) + np.where(n < exact, n, large)).astype(np.int32)


def _band_bias(t5_table, heads, radius, dil, tq=128):
    win = tq + 2 * radius
    period = win + tq
    j = np.arange(period)
    offs = np.array([0, -radius, -2 * radius])
    rel = offs[:, None] + np.where(j < win, j, j - period)[None, :]
    valid = np.abs(rel) <= radius
    bucket = _t5_bucket(dil * np.clip(rel, -radius, radius))
    g = jnp.transpose(t5_table[:, np.asarray(heads)][bucket], (0, 2, 1)).astype(F32)
    g = jnp.where(valid[:, None, :], g, NEG_INF)
    flat = jnp.tile(g, (1, 1, tq))[:, :, :tq * (period - 1)]
    tiles = flat.reshape(3, len(heads), tq, period - 1)[:, :, :, :win]
    return tiles.reshape(3, len(heads) // 2, 2 * tq, win) * LOG2E


def _mla_prep_kernel(cq_ref, ckv_ref, kr_ref, krs_ref, ct_ref, st_ref, gq_ref, gkv_ref,
                     wq1_ref, wq2_ref, wk_ref, wv_ref, q_ref, k_ref, v_ref):
    ct = jnp.concatenate([ct_ref[...]] * B_HEADS, axis=-1)
    st = jnp.concatenate([st_ref[...]] * B_HEADS, axis=-1)
    cn = _rms(cq_ref[...], gq_ref[...]).astype(BF16)
    q = _dot(cn, wq1_ref[...]) * ct + _dot(cn, wq2_ref[...]) * st
    q_ref[...] = q.astype(BF16)
    kvn = _rms(ckv_ref[...], gkv_ref[...]).astype(BF16)
    krope = kr_ref[...] * ct_ref[...] + krs_ref[...] * st_ref[...]
    k = _dot(kvn, wk_ref[...]) + jnp.concatenate([krope] * B_HEADS, axis=-1)
    k_ref[...] = k.astype(BF16)
    v_ref[...] = _dot(kvn, wv_ref[...]).astype(BF16)


def _mla_prep(lat, ct, st, gq, gkv, wq1, wq2, wk, wv, seq, tm):
    n = lat.shape[0]
    hw = B_HEADS * MLA_HEAD_PAD
    tps = seq // tm
    tab = pl.BlockSpec((tm, LANES), lambda i: (i % tps, 0))
    return pl.pallas_call(
        _mla_prep_kernel,
        grid=(n // tm,),
        in_specs=[
            pl.BlockSpec((tm, B_Q_RANK), lambda i: (i, LAT_CQ // B_Q_RANK)),
            pl.BlockSpec((tm, LANES), lambda i: (i, LAT_CKV // LANES)),
            pl.BlockSpec((tm, LANES), lambda i: (i, LAT_KR // LANES)),
            pl.BlockSpec((tm, LANES), lambda i: (i, LAT_KRS // LANES)),
            tab, tab,
            _resident(gq.shape), _resident(gkv.shape),
            _resident(wq1.shape), _resident(wq2.shape), _resident(wk.shape), _resident(wv.shape),
        ],
        out_specs=[pl.BlockSpec((tm, hw), lambda i: (i, 0))] * 3,
        out_shape=[jax.ShapeDtypeStruct((n, hw), BF16)] * 3,
        compiler_params=_params("parallel"),
        name="mla_prep",
    )(lat, lat, lat, lat, ct, st, gq, gkv, wq1, wq2, wk, wv)


def _mla_attn_kernel(q_ref, k_ref, v_ref, o_ref, *, seq, tq, tk):
    scale = (B_NOPE_DIM + B_ROPE_DIM) ** -0.5 * LOG2E
    heads =[slice(h * MLA_HEAD_PAD, (h + 1) * MLA_HEAD_PAD) for h in range(B_HEADS)]
    q = [q_ref[0, :, hs] for hs in heads]

    def body(c, carry):
        k0 = pl.multiple_of(c * tk, tk)
        new = []
        for hs, qh, (m, l, acc) in zip(heads, q, carry):
            kc = k_ref[0, pl.ds(k0, tk), hs]
            vc = v_ref[0, pl.ds(k0, tk), hs]
            s = _dot_nt(qh, kc) * scale
            m_new = jnp.maximum(m, jnp.max(s, axis=-1, keepdims=True))
            alpha = jnp.exp2(m - m_new)
            p = jnp.exp2(s - m_new)
            l = alpha * l + jnp.sum(p, axis=-1, keepdims=True)
            acc = alpha * acc + _dot(p.astype(BF16), vc)
            new.append((m_new, l, acc))
        return tuple(new)

    init = (jnp.full((tq, 1), -jnp.inf, F32), jnp.zeros((tq, 1), F32), jnp.zeros((tq, MLA_HEAD_PAD), F32))
    final = lax.fori_loop(0, seq // tk, body, (init,) * B_HEADS)
    outs = [(acc / l)[:, :B_V_DIM] for _, l, acc in final]
    o_ref[0] = jnp.concatenate(outs, axis=-1).astype(o_ref.dtype)


def _mla_attn(q, k, v, tq, tk):
    nb, seq, hw = q.shape
    return pl.pallas_call(
        functools.partial(_mla_attn_kernel, seq=seq, tq=tq, tk=tk),
        grid=(nb, seq // tq),
        in_specs=[
            pl.BlockSpec((1, tq, hw), lambda b, i: (b, i, 0)),
            pl.BlockSpec((1, seq, hw), lambda b, i: (b, 0, 0)),
            pl.BlockSpec((1, seq, hw), lambda b, i: (b, 0, 0)),
        ],
        out_specs=pl.BlockSpec((1, tq, BRANCH_W), lambda b, i: (b, i, 0)),
        out_shape=jax.ShapeDtypeStruct((nb, seq, BRANCH_W), BF16),
        compiler_params=_params("parallel", "parallel"),
        name="mla_attn",
    )(q, k, v)


def _rope_tables(seq):
    half = B_ROPE_DIM // 2
    inv = ROPE_THETA ** (-jnp.arange(half, dtype=F32) / half)
    ang = jnp.arange(seq, dtype=F32)[:, None] * inv[None, :]
    cos, sin = jnp.cos(ang), jnp.sin(ang)
    pad = jnp.zeros((seq, MLA_HEAD_PAD - B_NOPE_DIM - B_ROPE_DIM), F32)
    ct = jnp.concatenate([jnp.ones((seq, B_NOPE_DIM), F32), cos, cos, pad], axis=-1)
    st = jnp.concatenate([jnp.zeros((seq, B_NOPE_DIM), F32), -sin, sin, pad], axis=-1)
    return ct, st


def _mla_weights(w_uq, w_ukv):
    half = B_ROPE_DIM // 2
    wq = w_uq.reshape(B_Q_RANK, B_HEADS, B_NOPE_DIM + B_ROPE_DIM)
    nope, r1, r2 = wq[..., :B_NOPE_DIM], wq[..., B_NOPE_DIM:B_NOPE_DIM + half], wq[..., B_NOPE_DIM + half:]
    zpad = jnp.zeros((B_Q_RANK, B_HEADS, MLA_HEAD_PAD - B_NOPE_DIM - B_ROPE_DIM), F32)
    wq1 = jnp.concatenate([nope, r1, r2, zpad], axis=-1).reshape(B_Q_RANK, -1)
    wq2 = jnp.concatenate([jnp.zeros_like(nope), r2, r1, zpad], axis=-1).reshape(B_Q_RANK, -1)
    wkv = w_ukv.reshape(B_KV_RANK, B_HEADS, B_NOPE_DIM + B_V_DIM)
    z64 = jnp.zeros((B_KV_RANK, B_HEADS, MLA_HEAD_PAD - B_NOPE_DIM), F32)
    wk = jnp.concatenate([wkv[..., :B_NOPE_DIM], z64], axis=-1).reshape(B_KV_RANK, -1)
    wv = jnp.concatenate([wkv[..., B_NOPE_DIM:], z64], axis=-1).reshape(B_KV_RANK, -1)
    return tuple(w.astype(BF16) for w in (wq1, wq2, wk, wv))


def _na_kernel(bias_ref, q_ref, k_ref, v_ref, o_ref, *, rows):
    kh = NA_ROWS
    win = kh * GRID_W
    tq = GRID_W
    lo = lax.broadcasted_iota(jnp.int32, (tq, LANES), 1) < HEAD_DIM

    ub = 4
    npair = C_HEADS // 2
    units = [(u, j) for u in range(ub) for j in range(npair)]
    lanes = [slice(j * LANES, (j + 1) * LANES) for j in range(npair)]

    def body(step, carry):
        r = [step * ub + u for u in range(ub)]
        q0 = [pl.multiple_of(ri * tq, tq) for ri in r]
        r0 = [jnp.clip(ri - kh // 2, 0, rows - kh) for ri in r]
        ks = [pl.multiple_of(x * GRID_W, GRID_W) for x in r0]
        s = {}
        for u, j in units:
            q2 = _pair_rows(q_ref[0, pl.ds(q0[u], tq), lanes[j]], lo)
            s[u, j] = _dot_nt(q2, k_ref[0, pl.ds(ks[u], win), lanes[j]]) * LOG2E + bias_ref[r[u] - r0[u], j]
        soft = {uj: _pair_softmax(s[uj], None) for uj in units}
        for u in range(ub):
            outs = []
            for j in range(npair):
                p, _, l = soft[u, j]
                pv = _dot(p.astype(BF16), v_ref[0, pl.ds(ks[u], win), lanes[j]]) / l
                outs.append(jnp.where(lo, pv[:tq], pv[tq:]))
            o_ref[0, pl.ds(q0[u], tq), :] = jnp.concatenate(outs, axis=-1).astype(o_ref.dtype)
        return carry

    assert rows % ub == 0
    lax.fori_loop(0, rows // ub, body, 0)


def _na_attention(bias, src, seq, qcol):
    nb = src.shape[0]
    rows = seq // GRID_W
    assert rows >= NA_ROWS
    spec = lambda c: pl.BlockSpec((1, seq, BRANCH_W), lambda b: (b, 0, c))
    return pl.pallas_call(
        functools.partial(_na_kernel, rows=rows),
        grid=(nb,),
        in_specs=[_resident(bias.shape), spec(qcol), spec(qcol + 1), spec(qcol + 2)],
        out_specs=pl.BlockSpec((1, seq, BRANCH_W), lambda b: (b, 0, 0)),
        out_shape=jax.ShapeDtypeStruct((nb, seq, BRANCH_W), BF16),
        compiler_params=_params("parallel"),
        name="na_attn",
    )(bias, src, src, src)


def _na_bias(rpb):
    kh = NA_ROWS
    qc = np.arange(GRID_W)[:, None]
    kc = np.arange(GRID_W)[None, :]
    start = np.clip(qc - NA_COLS // 2, 0, GRID_W - NA_COLS)
    col_ok = (kc >= start) & (kc < start + NA_COLS)
    dc = np.clip(kc - qc + NA_COLS - 1, 0, 2 * NA_COLS - 2)
    v = np.arange(kh)[:, None]
    dr = np.clip(np.arange(kh)[None, :] - v + NA_ROWS - 1, 0, 2 * NA_ROWS - 2)
    b = rpb[:, dr][..., dc]
    b = jnp.where(col_ok[None, None, None], b.astype(F32), NEG_INF)
    b = jnp.transpose(b, (1, 0, 3, 2, 4))
    return b.reshape(kh, rpb.shape[0] // 2, 2 * GRID_W, kh * GRID_W) * LOG2E


def _merge_kernel(x_ref, g_ref, oa0_ref, oa1_ref, oa2_ref, sa0_ref, sa1_ref, sa2_ref, yb_ref, yc_ref, yd_ref,
                  wg_ref, wb_ref, wo_ref, out_ref):
    x = x_ref[...]
    h = _rms(x, g_ref[...]).astype(BF16)

    o = [oa0_ref[...], oa1_ref[...], oa2_ref[...]]
    st = [sa0_ref[...], sa1_ref[...], sa2_ref[...]]
    m = [s[:, 0:A_HEADS] for s in st]
    l = [s[:, A_HEADS:2 * A_HEADS] for s in st]
    m_all = jnp.maximum(jnp.maximum(m[0], m[1]), m[2])
    wgt = [l[g] * jnp.exp2(m[g] - m_all) for g in range(N_DIL)]
    den = wgt[0] + wgt[1] + wgt[2]
    parts = []
    for hh in range(A_HEADS):
        hs = slice(hh * HEAD_DIM, (hh + 1) * HEAD_DIM)
        num = wgt[0][:, hh:hh + 1] * o[0][:, hs]
        for g in range(1, N_DIL):
            num = num + wgt[g][:, hh:hh + 1] * o[g][:, hs]
        parts.append(num / den[:, hh:hh + 1])
    ya = jnp.concatenate(parts, axis=-1).astype(BF16)

    branches = [ya, yb_ref[...], yc_ref[...], yd_ref[...]]
    merged = None
    for i in range(N_BRANCH):
        gate = jax.nn.sigmoid(_dot(h, wg_ref[i]))
        term = gate * _dot(branches[i], wb_ref[i])
        merged = term if merged is None else merged + term
    out_ref[...] = x + _dot(merged.astype(BF16), wo_ref[...])


def _merge(x2, g, oa, sa, yb, yc, yd, wg, wb, wo, tm):
    n = x2.shape[0]
    row = lambda w: pl.BlockSpec((tm, w), lambda i: (i, 0))
    return pl.pallas_call(
        _merge_kernel,
        grid=(n // tm,),
        in_specs=[row(D_MODEL), _resident(g.shape)] + [row(BRANCH_W)] * 3 + [row(LANES)] * 3
                 + [row(BRANCH_W)] * 3 + [_resident(wg.shape), _resident(wb.shape), _resident(wo.shape)],
        out_specs=row(D_MODEL),
        out_shape=jax.ShapeDtypeStruct((n, D_MODEL), F32),
        compiler_params=_params("parallel"),
        name="merge",
    )(x2, g, *oa, *sa, yb, yc, yd, wg, wb, wo)


def _gelu_tanh(x):
    return x * (0.5 * (1.0 + jnp.tanh(math.sqrt(2.0 / math.pi) * (x + 0.044715 * (x * x * x)))))


def _ffn_kernel(*refs, tm, tiles_per_seq, nchunk, final):
    refs = list(refs)
    xp_ref, x_ref, xn_ref, g_ref, wg_ref, wu_ref, cw_ref, cb_ref, wd_ref = refs[:9]
    fg_ref = refs[9] if final else None
    out_ref, acc_ref = refs[-2:]
    i = pl.program_id(0)
    t = i % tiles_per_seq
    keep_prev = (t != 0).astype(F32)
    keep_next = (t != tiles_per_seq - 1).astype(F32)
    x = x_ref[...]
    g = g_ref[...]
    h = _rms(x, g).astype(BF16)
    hp = _rms(xp_ref[...], g).astype(BF16)
    hn = _rms(xn_ref[...], g).astype(BF16)
    row = lax.broadcasted_iota(jnp.int32, (tm, 1), 0)
    acc_ref[...] = jnp.zeros_like(acc_ref)

    def body(c, carry):
        wg = wg_ref[c]
        gate = _dot(h, wg)
        gp = _dot(hp, wg)[SUBLANES - 1:SUBLANES, :] * keep_prev
        gn = _dot(hn, wg)[0:1, :] * keep_next
        prev = jnp.where(row == 0, gp, pltpu.roll(gate, 1, axis=0))
        nxt = jnp.where(row == tm - 1, gn, pltpu.roll(gate, tm - 1, axis=0))
        cw = cw_ref[c]
        conv = cw[0:1, :] * prev + cw[1:2, :] * gate + cw[2:3, :] * nxt + cb_ref[c]
        u = _gelu_tanh(conv) * _dot(h, wu_ref[c])
        acc_ref[...] += _dot(u.astype(BF16), wd_ref[c])
        return carry

    lax.fori_loop(0, nchunk, body, 0)
    y = x + acc_ref[...]
    if final:
        y = _rms(y, fg_ref[...])
    out_ref[...] = y


def _ffn(x2, g, wg, wu, cw, cb, wd, final_g, seq, tm):
    n = x2.shape[0]
    nchunk = wg.shape[0]
    nb8 = n // SUBLANES
    step = tm // SUBLANES
    final = final_g is not None
    in_specs = [
        pl.BlockSpec((SUBLANES, D_MODEL), lambda i: (jnp.maximum(i * step - 1, 0), 0)),
        pl.BlockSpec((tm, D_MODEL), lambda i: (i, 0)),
        pl.BlockSpec((SUBLANES, D_MODEL), lambda i: (jnp.minimum((i + 1) * step, nb8 - 1), 0)),
        _resident(g.shape), _resident(wg.shape), _resident(wu.shape), _resident(cw.shape),
        _resident(cb.shape), _resident(wd.shape),
    ]
    args = [x2, x2, x2, g, wg, wu, cw, cb, wd]
    if final:
        in_specs.append(_resident(final_g.shape))
        args.append(final_g)
    return pl.pallas_call(
        functools.partial(_ffn_kernel, tm=tm, tiles_per_seq=seq // tm, nchunk=nchunk, final=final),
        grid=(n // tm,),
        in_specs=in_specs,
        out_specs=pl.BlockSpec((tm, D_MODEL), lambda i: (i, 0)),
        out_shape=jax.ShapeDtypeStruct((n, D_MODEL), F32),
        scratch_shapes=[pltpu.VMEM((tm, D_MODEL), F32)],
        compiler_params=_params("parallel"),
        name="ffn",
    )(*args)


def _proj_in_weight(w):
    a, cq, ckv, kr, c, dq, dk, dv = jnp.split(w, np.cumsum(
        [A_COLS, B_Q_RANK, B_KV_RANK, B_ROPE_DIM, C_COLS, D_Q_HEADS * HEAD_DIM, D_KV_HEADS * HEAD_DIM])[:], axis=1)
    half = B_ROPE_DIM // 2
    z = lambda width: jnp.zeros((D_MODEL, width), w.dtype)
    kr_blk = jnp.concatenate([z(B_NOPE_DIM), kr, z(LANES - B_NOPE_DIM - B_ROPE_DIM)], axis=1)
    krs_blk = jnp.concatenate([z(B_NOPE_DIM), kr[:, half:], kr[:, :half], z(LANES - B_NOPE_DIM - B_ROPE_DIM)], axis=1)
    dq = _permute_heads(dq, D_HEAD_ORDER, axis=1) * QK_SCALE
    nq = N_DIL * A_HEADS * HEAD_DIM
    a = jnp.concatenate([a[:, :nq] * QK_SCALE, a[:, nq:]], axis=1)
    c = jnp.concatenate([c[:, :BRANCH_W] * QK_SCALE, c[:, BRANCH_W:]], axis=1)
    return jnp.concatenate([a, c, dq, dk, dv, cq, ckv, kr_blk, krs_blk], axis=1).astype(BF16)


D_HEAD_ORDER = (0, 2, 1, 3)


def _permute_heads(w, order, axis):
    shape = w.shape
    split = shape[:axis] + (len(order), HEAD_DIM) + shape[axis + 1:]
    return jnp.take(w.reshape(split), np.asarray(order), axis=axis).reshape(shape)


def _ffn_chunks(w, axis, fc):
    if axis == 1:
        return jnp.transpose(w.reshape(w.shape[0], D_FF // fc, fc), (1, 0, 2))
    return w.reshape(D_FF // fc, fc, w.shape[1])


def _layer(x2, nb, seq, t5_table, d_bias, ct, st, p, final_g):
    n = x2.shape[0]
    tm = 512
    main, lat = _proj_in(x2, p["norm_mix_g"], p["w_in"], tm)

    oa, sa = [], []
    for gi, (window, dil) in enumerate(DIL_CFG):
        radius = window // dil // 2
        bias = _band_bias(t5_table, range(gi * A_HEADS, (gi + 1) * A_HEADS), radius, dil)
        src = main.reshape(nb, seq // dil, dil * MAIN_COLS)
        o, s = _band_attention(bias, None, src, nstream=dil, seq=seq // dil, cols=MAIN_COLS,
                               qcol=gi, kcol=N_DIL + gi, vcol=2 * N_DIL + gi, kvw=BRANCH_W, radius=radius,
                               out_dtype=F32, emit_stats=True, name=f"band_a{gi}")
        oa.append(o.reshape(n, BRANCH_W))
        sa.append(s.reshape(n, LANES))

    q, k, v = _mla_prep(lat, ct, st, p["q_norm_g"], p["kv_norm_g"], *p["mla_w"], seq, tm)
    hw = B_HEADS * MLA_HEAD_PAD
    yb = _mla_attn(q.reshape(nb, seq, hw), k.reshape(nb, seq, hw), v.reshape(nb, seq, hw), 512, 512)

    src = main.reshape(nb, seq, MAIN_COLS)
    yc = _na_attention(_na_bias(p["na_bias"]), src, seq, MAIN_C // BRANCH_W)

    yd = _band_attention(d_bias, p["sink_logit"], src, nstream=1, seq=seq, cols=MAIN_COLS,
                         qcol=MAIN_DQ // BRANCH_W, kcol=MAIN_DK // LANES, vcol=MAIN_DV // LANES, kvw=LANES,
                         radius=D_RADIUS, out_dtype=BF16, emit_stats=False, name="band_d")

    x2 = _merge(x2, p["norm_mix_g"], oa, sa, yb.reshape(n, BRANCH_W), yc.reshape(n, BRANCH_W),
                yd.reshape(n, BRANCH_W), p["w_gate"], p["w_branch"], p["w_out"], tm)
    return _ffn(x2, p["norm_ffn_g"], p["w_ffn_gate"], p["w_ffn_up"], p["conv_w"], p["conv_b"],
                p["w_ffn_down"], final_g, seq, tm)


def kernel(x, t5_table, norm_mix_g, w_in, q_norm_g, w_uq, kv_norm_g, w_ukv, na_bias, sink_logit, w_gate, w_branch, w_out, norm_ffn_g, w_ffn_gate, w_ffn_up, conv_w, conv_b, w_ffn_down, final_g):
    nb, seq, _ = x.shape
    depth = w_in.shape[0]
    fc = 256
    x2 = x.reshape(nb * seq, D_MODEL)
    ct, st = _rope_tables(seq)
    d_bias = _band_bias(t5_table, [N_DIL * A_HEADS + h for h in D_HEAD_ORDER], D_RADIUS, 1)
    for layer in range(depth):
        wb = w_branch[layer]
        wb = jnp.concatenate([wb[:3], _permute_heads(wb[3], D_HEAD_ORDER, axis=0)[None]], axis=0)
        p = {
            "norm_mix_g": norm_mix_g[layer][None, :],
            "w_in": _proj_in_weight(w_in[layer]),
            "q_norm_g": q_norm_g[layer][None, :],
            "kv_norm_g": kv_norm_g[layer][None, :],
            "mla_w": _mla_weights(w_uq[layer], w_ukv[layer]),
            "na_bias": na_bias[layer],
            "sink_logit": sink_logit[layer][np.asarray(D_HEAD_ORDER)] * LOG2E,
            "w_gate": w_gate[layer].astype(BF16),
            "w_branch": wb.astype(BF16),
            "w_out": w_out[layer].astype(BF16),
            "norm_ffn_g": norm_ffn_g[layer][None, :],
            "w_ffn_gate": _ffn_chunks(w_ffn_gate[layer], 1, fc).astype(BF16),
            "w_ffn_up": _ffn_chunks(w_ffn_up[layer], 1, fc).astype(BF16),
            "conv_w": _ffn_chunks(conv_w[layer], 1, fc),
            "conv_b": conv_b[layer].reshape(D_FF // fc, 1, fc),
            "w_ffn_down": _ffn_chunks(w_ffn_down[layer], 0, fc).astype(BF16),
        }
        x2 = _layer(x2, nb, seq, t5_table, d_bias, ct, st, p,
                    final_g[None, :] if layer == depth - 1 else None)
    return x2.reshape(nb, seq, D_MODEL)
```

```python
import functools
import math

import jax
import jax.numpy as jnp
import numpy as np
from jax import lax
from jax.experimental import pallas as pl
from jax.experimental.pallas import tpu as pltpu

D_MODEL = 1024
GRID_W = 64
RMS_EPS = 1e-6
NEG_INF = -1e30
N_BRANCH = 4
BRANCH_W = 256

DIL_CFG = ((128, 1), (512, 4), (2048, 16))
N_DIL = 3
A_HEADS = 4
HEAD_DIM = 64
B_HEADS = 4
B_Q_RANK = 256
B_KV_RANK = 128
B_NOPE_DIM = 64
B_ROPE_DIM = 32
B_V_DIM = 64
ROPE_THETA = 10000.0
C_HEADS = 4
NA_ROWS = 8
NA_COLS = 16
D_Q_HEADS = 4
D_KV_HEADS = 2
D_RADIUS = 128
T5_BUCKETS = 32
T5_MAX_DIST = 1024
T5_HEADS = N_DIL * A_HEADS + D_Q_HEADS
D_FF = 2816
CONV_W = 3

A_COLS = 3 * N_DIL * A_HEADS * HEAD_DIM
C_COLS = 3 * C_HEADS * HEAD_DIM

LANES = 128
SUBLANES = 8
VMEM_LIMIT_BYTES = 56 * 1024 * 1024

QKV_COLS = 3 * BRANCH_W
MAIN_A0 = 0
MAIN_C = MAIN_A0 + QKV_COLS
MAIN_DQ = MAIN_C + QKV_COLS
MAIN_DK = MAIN_DQ + 256
MAIN_DV = MAIN_DK + 128
MAIN_COLS = MAIN_DV + 128
DIL_STREAMS = tuple(dil for _, dil in DIL_CFG[1:])
DIL_BASE = tuple(MAIN_COLS + i * QKV_COLS for i in range(len(DIL_STREAMS)))
LAT_BASE = MAIN_COLS + len(DIL_STREAMS) * QKV_COLS
LAT_CQ = 0
LAT_CKV = 256
LAT_KR = 384
LAT_KRS = 512
LAT_COLS = 640

MLA_HEAD_PAD = 128

BF16 = jnp.bfloat16
F32 = jnp.float32
LOG2E = math.log2(math.e)
QK_SCALE = HEAD_DIM ** -0.5
assert math.frexp(QK_SCALE)[0] == 0.5
MLA_LOGIT_SCALE = (B_NOPE_DIM + B_ROPE_DIM) ** -0.5 * LOG2E


def _rms(x, g):
    return x * lax.rsqrt(jnp.mean(x * x, axis=-1, keepdims=True) + RMS_EPS) * g


def _dot(a, b):
    return jnp.dot(a, b, preferred_element_type=F32)


def _dot_nt(a, b):
    return lax.dot_general(a, b, (((1,), (1,)), ((), ())), preferred_element_type=F32)


def _params(*sem):
    return pltpu.CompilerParams(dimension_semantics=sem, vmem_limit_bytes=VMEM_LIMIT_BYTES)


def _resident(shape):
    nd = len(shape)
    return pl.BlockSpec(shape, lambda *_: (0,) * nd)


def _proj_in_kernel(x_ref, g_ref, w_ref, main_ref, *rest, tm, chunk):
    nd = len(DIL_STREAMS)
    dil_refs, lat_ref, scr_refs = rest[:nd], rest[nd], rest[nd + 1:]
    h = _rms(x_ref[...], g_ref[...]).astype(BF16)
    for c0 in range(0, MAIN_COLS, chunk):
        main_ref[:, c0:c0 + chunk] = _dot(h, w_ref[:, c0:c0 + chunk]).astype(BF16)
    nslab = QKV_COLS // LANES
    for out_ref, scr_ref, base, dil in zip(dil_refs, scr_refs, DIL_BASE, DIL_STREAMS):
        res = _dot(h, w_ref[:, base:base + QKV_COLS])
        for k in range(nslab):
            scr_ref[k] = res[:, k * LANES:(k + 1) * LANES]
        for r in range(dil):
            rows = [scr_ref[k, pl.ds(r, tm // dil, stride=dil), :] for k in range(nslab)]
            out_ref[0, r] = jnp.concatenate(rows, axis=-1).astype(BF16)
    lat_ref[...] = _dot(h, w_ref[:, LAT_BASE:])


def _proj_in(x2, g, w, nb, seq, tm):
    n = x2.shape[0]
    tps = seq // tm
    row = lambda width: pl.BlockSpec((tm, width), lambda i: (i, 0))
    dil_specs = [pl.BlockSpec((1, d, tm // d, QKV_COLS), lambda i: (i // tps, 0, i % tps, 0)) for d in DIL_STREAMS]
    dil_shapes = [jax.ShapeDtypeStruct((nb, d, seq // d, QKV_COLS), BF16) for d in DIL_STREAMS]
    outs = pl.pallas_call(
        functools.partial(_proj_in_kernel, tm=tm, chunk=512),
        grid=(n // tm,),
        in_specs=[row(D_MODEL), _resident((1, D_MODEL)), _resident(w.shape)],
        out_specs=[row(MAIN_COLS)] + dil_specs + [row(LAT_COLS)],
        out_shape=[jax.ShapeDtypeStruct((n, MAIN_COLS), BF16)] + dil_shapes
                  + [jax.ShapeDtypeStruct((n, LAT_COLS), F32)],
        scratch_shapes=[pltpu.VMEM((QKV_COLS // LANES, tm, LANES), F32) for _ in DIL_STREAMS],
        compiler_params=_params("parallel"),
        name="proj_in",
    )(x2, g, w)
    return outs[0], outs[1:-1], outs[-1]


def _pair_rows(qv, lo):
    zero = jnp.zeros_like(qv)
    return jnp.concatenate([jnp.where(lo, qv, zero), jnp.where(lo, zero, qv)], axis=0)


def _pair_softmax(s, sink):
    m = jnp.max(s, axis=-1, keepdims=True)
    if sink is not None:
        m = jnp.maximum(m, sink)
    p = jnp.exp2(s - m)
    l = jnp.sum(p, axis=-1, keepdims=True)
    if sink is not None:
        l = l + jnp.exp2(sink - m)
    return p, m, l


def _band_kernel(*refs, seq, tq, radius, npair, nkv, has_sink, emit_stats):
    refs = list(refs)
    bias_ref = refs.pop(0)
    sink_ref = refs.pop(0) if has_sink else None
    q_ref, k_ref, v_ref, o_ref = refs[:4]
    st_ref = refs[4] if emit_stats else None
    win = tq + 2 * radius
    nblk = seq // tq
    lane = lax.broadcasted_iota(jnp.int32, (tq, LANES), 1)
    lo = lane < HEAD_DIM
    top = lax.broadcasted_iota(jnp.int32, (2 * tq, 1), 0) < tq

    ub = min(nblk, 4)
    assert nblk % ub == 0
    units = [(u, j) for u in range(ub) for j in range(npair)]
    kv_lanes = [slice((j if nkv == npair else 0) * LANES, ((j if nkv == npair else 0) + 1) * LANES)
                for j in range(npair)]

    def body(step, carry):
        q0 = [pl.multiple_of((step * ub + u) * tq, tq) for u in range(ub)]
        ks = [pl.multiple_of(jnp.clip(q - radius, 0, seq - win), radius) for q in q0]
        variant = [jnp.where(step * ub + u == 0, 0, jnp.where(step * ub + u == nblk - 1, 2, 1)) for u in range(ub)]
        s = {}
        for u, j in units:
            q2 = _pair_rows(q_ref[0, pl.ds(q0[u], tq), j * LANES:(j + 1) * LANES], lo)
            s[u, j] = _dot_nt(q2, k_ref[0, pl.ds(ks[u], win), kv_lanes[j]]) * LOG2E + bias_ref[variant[u], j]
        soft = {}
        for u, j in units:
            sink = jnp.where(top, sink_ref[2 * j], sink_ref[2 * j + 1]) if has_sink else None
            soft[u, j] = _pair_softmax(s[u, j], sink)
        for u in range(ub):
            outs = []
            stats = jnp.zeros((tq, LANES), F32)
            for j in range(npair):
                p, m, l = soft[u, j]
                pv = _dot(p.astype(BF16), v_ref[0, pl.ds(ks[u], win), kv_lanes[j]]) / l
                outs.append(jnp.where(lo, pv[:tq], pv[tq:]))
                if emit_stats:
                    for half, rows in enumerate((slice(0, tq), slice(tq, 2 * tq))):
                        stats = jnp.where(lane == 2 * j + half, m[rows], stats)
                        stats = jnp.where(lane == 2 * npair + 2 * j + half, l[rows], stats)
            o_ref[0, pl.ds(q0[u], tq), :] = jnp.concatenate(outs, axis=-1).astype(o_ref.dtype)
            if emit_stats:
                st_ref[0, pl.ds(q0[u], tq), :] = stats
        return carry

    lax.fori_loop(0, nblk // ub, body, 0)


def _band_attention(bias, sink, src, *, nstream, seq, cols, qcol, kcol, vcol, kvw, radius,
                    out_dtype, emit_stats, name):
    nb = src.shape[0]
    npair = bias.shape[1]
    tq = bias.shape[2] // 2
    qw = npair * LANES
    assert seq % tq == 0 and seq >= tq + 2 * radius and tq >= radius and bias.shape[3] == tq + 2 * radius
    kernel = functools.partial(_band_kernel, seq=seq, tq=tq, radius=radius, npair=npair, nkv=kvw // LANES,
                               has_sink=sink is not None, emit_stats=emit_stats)
    in_specs = [_resident(bias.shape)]
    args = [bias]
    if sink is not None:
        in_specs.append(pl.BlockSpec(memory_space=pltpu.SMEM))
        args.append(sink)
    in_specs += [
        pl.BlockSpec((1, seq, qw), lambda b, r: (b, 0, r * (cols // qw) + qcol)),
        pl.BlockSpec((1, seq, kvw), lambda b, r: (b, 0, r * (cols // kvw) + kcol)),
        pl.BlockSpec((1, seq, kvw), lambda b, r: (b, 0, r * (cols // kvw) + vcol)),
    ]
    args += [src, src, src]
    out_specs = [pl.BlockSpec((1, seq, qw), lambda b, r: (b, 0, r))]
    out_shape = [jax.ShapeDtypeStruct((nb, seq, nstream * qw), out_dtype)]
    if emit_stats:
        out_specs.append(pl.BlockSpec((1, seq, LANES), lambda b, r: (b, 0, r)))
        out_shape.append(jax.ShapeDtypeStruct((nb, seq, nstream * LANES), F32))
    outs = pl.pallas_call(
        kernel,
        grid=(nb, nstream),
        in_specs=in_specs,
        out_specs=out_specs,
        out_shape=out_shape,
        compiler_params=_params("parallel", "parallel"),
        name=name,
    )(*args)
    return outs if emit_stats else outs[0]


def _t5_bucket(rel):
    half = T5_BUCKETS // 2
    exact = half // 2
    n = np.abs(rel)
    large = exact + (np.log(np.maximum(n, 1) / exact) / math.log(T5_MAX_DIST / exact)
                     * (half - exact)).astype(np.int32)
    large = np.minimum(large, half - 1)
    return (np.where(rel > 0, half, 0) + np.where(n < exact, n, large)).astype(np.int32)


def _band_bias(t5_table, heads, radius, dil, tq=128):
    win = tq + 2 * radius
    period = win + tq
    j = np.arange(period)
    offs = np.array([0, -radius, -2 * radius])
    rel = offs[:, None] + np.where(j < win, j, j - period)[None, :]
    valid = np.abs(rel) <= radius
    bucket = _t5_bucket(dil * np.clip(rel, -radius, radius))
    g = jnp.transpose(t5_table[:, np.asarray(heads)][bucket], (0, 2, 1)).astype(F32)
    g = jnp.where(valid[:, None, :], g, NEG_INF)
    flat = jnp.tile(g, (1, 1, tq))[:, :, :tq * (period - 1)]
    tiles = flat.reshape(3, len(heads), tq, period - 1)[:, :, :, :win]
    return tiles.reshape(3, len(heads) // 2, 2 * tq, win) * LOG2E


def _mla_prep_kernel(cq_ref, ckv_ref, kr_ref, krs_ref, ct_ref, st_ref, gq_ref, gkv_ref,
                     wq1_ref, wq2_ref, wk_ref, wv_ref, q_ref, k_ref, vt_ref):
    ct = jnp.concatenate([ct_ref[...]] * B_HEADS, axis=-1)
    st = jnp.concatenate([st_ref[...]] * B_HEADS, axis=-1)
    cn = _rms(cq_ref[...], gq_ref[...]).astype(BF16)
    q = _dot(cn, wq1_ref[...]) * ct + _dot(cn, wq2_ref[...]) * st
    q_ref[...] = (q * MLA_LOGIT_SCALE).astype(BF16)
    kvn = _rms(ckv_ref[...], gkv_ref[...]).astype(BF16)
    krope = kr_ref[...] * ct_ref[...] + krs_ref[...] * st_ref[...]
    k = _dot(kvn, wk_ref[...]) + jnp.concatenate([krope] * B_HEADS, axis=-1)
    k_ref[...] = k.astype(BF16)
    vt_ref[0, 0] = _dot(kvn, wv_ref[...]).T.astype(BF16)


def _mla_prep(lat, ct, st, gq, gkv, wq1, wq2, wk, wv, nb, seq, tm):
    n = lat.shape[0]
    hw = B_HEADS * MLA_HEAD_PAD
    vw = B_HEADS * B_V_DIM
    tps = seq // tm
    tab = pl.BlockSpec((tm, LANES), lambda i: (i % tps, 0))
    return pl.pallas_call(
        _mla_prep_kernel,
        grid=(n // tm,),
        in_specs=[
            pl.BlockSpec((tm, B_Q_RANK), lambda i: (i, LAT_CQ // B_Q_RANK)),
            pl.BlockSpec((tm, LANES), lambda i: (i, LAT_CKV // LANES)),
            pl.BlockSpec((tm, LANES), lambda i: (i, LAT_KR // LANES)),
            pl.BlockSpec((tm, LANES), lambda i: (i, LAT_KRS // LANES)),
            tab, tab,
            _resident(gq.shape), _resident(gkv.shape),
            _resident(wq1.shape), _resident(wq2.shape), _resident(wk.shape), _resident(wv.shape),
        ],
        out_specs=[pl.BlockSpec((tm, hw), lambda i: (i, 0))] * 2
                  + [pl.BlockSpec((1, 1, vw, tm), lambda i: (i // tps, i % tps, 0, 0))],
        out_shape=[jax.ShapeDtypeStruct((n, hw), BF16)] * 2
                  + [jax.ShapeDtypeStruct((nb, tps, vw, tm), BF16)],
        compiler_params=_params("parallel"),
        name="mla_prep",
    )(lat, lat, lat, lat, ct, st, gq, gkv, wq1, wq2, wk, wv)


def _mla_attn_kernel(q_ref, k_ref, vt_ref, o_ref, *, nchunk, tq):
    heads = range(B_HEADS)
    q = [q_ref[0, :, h * MLA_HEAD_PAD:(h + 1) * MLA_HEAD_PAD] for h in heads]

    uc = 2
    assert nchunk % uc == 0

    def body(step, carry):
        chunks = [step * uc + u for u in range(uc)]
        s = [[_dot_nt(k_ref[0, c, :, h * MLA_HEAD_PAD:(h + 1) * MLA_HEAD_PAD], q[h]) for h in heads]
             for c in chunks]
        for u, c in enumerate(chunks):
            stats = []
            for h in heads:
                m, l, _ = carry[h]
                m_new = jnp.maximum(m, jnp.max(s[u][h], axis=0, keepdims=True))
                alpha = jnp.exp2(m - m_new)
                p = jnp.exp2(s[u][h] - m_new)
                stats.append((m_new, alpha * l + jnp.sum(p, axis=0, keepdims=True), alpha, p.astype(BF16)))
            new = []
            for h in heads:
                m_new, l, alpha, p = stats[h]
                pv = _dot(vt_ref[0, c, h * B_V_DIM:(h + 1) * B_V_DIM, :], p)
                new.append((m_new, l, alpha * carry[h][2] + pv))
            carry = tuple(new)
        return carry

    init = (jnp.full((1, tq), -jnp.inf, F32), jnp.zeros((1, tq), F32), jnp.zeros((B_V_DIM, tq), F32))
    final = lax.fori_loop(0, nchunk // uc, body, (init,) * B_HEADS)
    out_t = jnp.concatenate([acc / l for _, l, acc in final], axis=0)
    o_ref[0] = out_t.T.astype(o_ref.dtype)


def _mla_attn(q, k, vt, tq):
    nb, seq, hw = q.shape
    nchunk = k.shape[1]
    return pl.pallas_call(
        functools.partial(_mla_attn_kernel, nchunk=nchunk, tq=tq),
        grid=(nb, seq // tq),
        in_specs=[
            pl.BlockSpec((1, tq, hw), lambda b, i: (b, i, 0)),
            pl.BlockSpec((1,) + k.shape[1:], lambda b, i: (b, 0, 0, 0)),
            pl.BlockSpec((1,) + vt.shape[1:], lambda b, i: (b, 0, 0, 0)),
        ],
        out_specs=pl.BlockSpec((1, tq, BRANCH_W), lambda b, i: (b, i, 0)),
        out_shape=jax.ShapeDtypeStruct((nb, seq, BRANCH_W), BF16),
        compiler_params=_params("parallel", "parallel"),
        name="mla_attn",
    )(q, k, vt)


def _rope_tables(seq):
    half = B_ROPE_DIM // 2
    inv = ROPE_THETA ** (-jnp.arange(half, dtype=F32) / half)
    ang = jnp.arange(seq, dtype=F32)[:, None] * inv[None, :]
    cos, sin = jnp.cos(ang), jnp.sin(ang)
    pad = jnp.zeros((seq, MLA_HEAD_PAD - B_NOPE_DIM - B_ROPE_DIM), F32)
    ct = jnp.concatenate([jnp.ones((seq, B_NOPE_DIM), F32), cos, cos, pad], axis=-1)
    st = jnp.concatenate([jnp.zeros((seq, B_NOPE_DIM), F32), -sin, sin, pad], axis=-1)
    return ct, st


def _mla_weights(w_uq, w_ukv):
    half = B_ROPE_DIM // 2
    wq = w_uq.reshape(B_Q_RANK, B_HEADS, B_NOPE_DIM + B_ROPE_DIM)
    nope, r1, r2 = wq[..., :B_NOPE_DIM], wq[..., B_NOPE_DIM:B_NOPE_DIM + half], wq[..., B_NOPE_DIM + half:]
    zpad = jnp.zeros((B_Q_RANK, B_HEADS, MLA_HEAD_PAD - B_NOPE_DIM - B_ROPE_DIM), F32)
    wq1 = jnp.concatenate([nope, r1, r2, zpad], axis=-1).reshape(B_Q_RANK, -1)
    wq2 = jnp.concatenate([jnp.zeros_like(nope), r2, r1, zpad], axis=-1).reshape(B_Q_RANK, -1)
    wkv = w_ukv.reshape(B_KV_RANK, B_HEADS, B_NOPE_DIM + B_V_DIM)
    z64 = jnp.zeros((B_KV_RANK, B_HEADS, MLA_HEAD_PAD - B_NOPE_DIM), F32)
    wk = jnp.concatenate([wkv[..., :B_NOPE_DIM], z64], axis=-1).reshape(B_KV_RANK, -1)
    wv = wkv[..., B_NOPE_DIM:].reshape(B_KV_RANK, -1)
    return tuple(w.astype(BF16) for w in (wq1, wq2, wk, wv))


def _na_kernel(bias_ref, q_ref, k_ref, v_ref, o_ref, *, rows):
    kh = NA_ROWS
    win = kh * GRID_W
    tq = GRID_W
    lo = lax.broadcasted_iota(jnp.int32, (tq, LANES), 1) < HEAD_DIM

    ub = 4
    npair = C_HEADS // 2
    units = [(u, j) for u in range(ub) for j in range(npair)]
    lanes = [slice(j * LANES, (j + 1) * LANES) for j in range(npair)]

    def body(step, carry):
        r = [step * ub + u for u in range(ub)]
        q0 = [pl.multiple_of(ri * tq, tq) for ri in r]
        r0 = [jnp.clip(ri - kh // 2, 0, rows - kh) for ri in r]
        ks = [pl.multiple_of(x * GRID_W, GRID_W) for x in r0]
        s = {}
        for u, j in units:
            q2 = _pair_rows(q_ref[0, pl.ds(q0[u], tq), lanes[j]], lo)
            s[u, j] = _dot_nt(q2, k_ref[0, pl.ds(ks[u], win), lanes[j]]) * LOG2E + bias_ref[r[u] - r0[u], j]
        soft = {uj: _pair_softmax(s[uj], None) for uj in units}
        for u in range(ub):
            outs = []
            for j in range(npair):
                p, _, l = soft[u, j]
                pv = _dot(p.astype(BF16), v_ref[0, pl.ds(ks[u], win), lanes[j]]) / l
                outs.append(jnp.where(lo, pv[:tq], pv[tq:]))
            o_ref[0, pl.ds(q0[u], tq), :] = jnp.concatenate(outs, axis=-1).astype(o_ref.dtype)
        return carry

    assert rows % ub == 0
    lax.fori_loop(0, rows // ub, body, 0)


def _na_attention(bias, src, seq, qcol):
    nb = src.shape[0]
    rows = seq // GRID_W
    assert rows >= NA_ROWS
    spec = lambda c: pl.BlockSpec((1, seq, BRANCH_W), lambda b: (b, 0, c))
    return pl.pallas_call(
        functools.partial(_na_kernel, rows=rows),
        grid=(nb,),
        in_specs=[_resident(bias.shape), spec(qcol), spec(qcol + 1), spec(qcol + 2)],
        out_specs=pl.BlockSpec((1, seq, BRANCH_W), lambda b: (b, 0, 0)),
        out_shape=jax.ShapeDtypeStruct((nb, seq, BRANCH_W), BF16),
        compiler_params=_params("parallel"),
        name="na_attn",
    )(bias, src, src, src)


def _na_bias(rpb):
    kh = NA_ROWS
    qc = np.arange(GRID_W)[:, None]
    kc = np.arange(GRID_W)[None, :]
    start = np.clip(qc - NA_COLS // 2, 0, GRID_W - NA_COLS)
    col_ok = (kc >= start) & (kc < start + NA_COLS)
    dc = np.clip(kc - qc + NA_COLS - 1, 0, 2 * NA_COLS - 2)
    v = np.arange(kh)[:, None]
    dr = np.clip(np.arange(kh)[None, :] - v + NA_ROWS - 1, 0, 2 * NA_ROWS - 2)
    b = rpb[:, dr][..., dc]
    b = jnp.where(col_ok[None, None, None], b.astype(F32), NEG_INF)
    b = jnp.transpose(b, (1, 0, 3, 2, 4))
    return b.reshape(kh, rpb.shape[0] // 2, 2 * GRID_W, kh * GRID_W) * LOG2E


def _merge_kernel(x_ref, g_ref, oa0_ref, oa1_ref, oa2_ref, sa0_ref, sa1_ref, sa2_ref, yb_ref, yc_ref, yd_ref,
                  wg_ref, wb_ref, wo_ref, out_ref, *scr_refs, tm):
    x = x_ref[...]
    h = _rms(x, g_ref[...]).astype(BF16)

    def token_order(ref, scr_ref, dil):
        nslab = scr_ref.shape[0]
        for r in range(dil):
            blk = ref[0, r]
            for k in range(nslab):
                scr_ref[k, pl.ds(r, tm // dil, stride=dil), :] = blk[:, k * LANES:(k + 1) * LANES]
        return jnp.concatenate([scr_ref[k] for k in range(nslab)], axis=-1)

    o = [oa0_ref[...]] + [token_order(ref, scr, d) for ref, scr, d in zip((oa1_ref, oa2_ref), scr_refs[:2], DIL_STREAMS)]
    st = [sa0_ref[...]] + [token_order(ref, scr, d) for ref, scr, d in zip((sa1_ref, sa2_ref), scr_refs[2:], DIL_STREAMS)]
    m = [s[:, 0:A_HEADS] for s in st]
    l = [s[:, A_HEADS:2 * A_HEADS] for s in st]
    m_all = jnp.maximum(jnp.maximum(m[0], m[1]), m[2])
    wgt = [l[g] * jnp.exp2(m[g] - m_all) for g in range(N_DIL)]
    den = wgt[0] + wgt[1] + wgt[2]
    parts = []
    for hh in range(A_HEADS):
        hs = slice(hh * HEAD_DIM, (hh + 1) * HEAD_DIM)
        num = wgt[0][:, hh:hh + 1] * o[0][:, hs]
        for g in range(1, N_DIL):
            num = num + wgt[g][:, hh:hh + 1] * o[g][:, hs]
        parts.append(num / den[:, hh:hh + 1])
    ya = jnp.concatenate(parts, axis=-1).astype(BF16)

    branches = [ya, yb_ref[...], yc_ref[...], yd_ref[...]]
    logits = _dot(h, wg_ref[0])
    merged = None
    for i in range(N_BRANCH):
        ahead = _dot(h, wg_ref[i + 1]) if i + 1 < N_BRANCH else None
        term = jax.nn.sigmoid(logits) * _dot(branches[i], wb_ref[i])
        merged = term if merged is None else merged + term
        logits = ahead
    out_ref[...] = x + _dot(merged.astype(BF16), wo_ref[...])


def _merge(x2, g, oa, sa, yb, yc, yd, wg, wb, wo, seq, tm):
    n = x2.shape[0]
    tps = seq // tm
    row = lambda w: pl.BlockSpec((tm, w), lambda i: (i, 0))
    streams = lambda w: [pl.BlockSpec((1, d, tm // d, w), lambda i: (i // tps, 0, i % tps, 0)) for d in DIL_STREAMS]
    return pl.pallas_call(
        functools.partial(_merge_kernel, tm=tm),
        grid=(n // tm,),
        in_specs=[row(D_MODEL), _resident(g.shape), row(BRANCH_W)] + streams(BRANCH_W) + [row(LANES)]
                 + streams(LANES) + [row(BRANCH_W)] * 3
                 + [_resident(wg.shape), _resident(wb.shape), _resident(wo.shape)],
        out_specs=row(D_MODEL),
        out_shape=jax.ShapeDtypeStruct((n, D_MODEL), F32),
        scratch_shapes=[pltpu.VMEM((BRANCH_W // LANES, tm, LANES), F32) for _ in DIL_STREAMS]
                       + [pltpu.VMEM((1, tm, LANES), F32) for _ in DIL_STREAMS],
        compiler_params=_params("parallel"),
        name="merge",
    )(x2, g, *oa, *sa, yb, yc, yd, wg, wb, wo)


def _gelu_tanh(x):
    return x * (0.5 * (1.0 + jnp.tanh(math.sqrt(2.0 / math.pi) * (x + 0.044715 * (x * x * x)))))


def _ffn_kernel(*refs, tm, tiles_per_seq, nchunk, final):
    refs = list(refs)
    xp_ref, x_ref, xn_ref, g_ref, wg_ref, wu_ref, cw_ref, cb_ref, wd_ref = refs[:9]
    fg_ref = refs[9] if final else None
    out_ref, acc_ref = refs[-2:]
    i = pl.program_id(0)
    t = i % tiles_per_seq
    keep_prev = (t != 0).astype(F32)
    keep_next = (t != tiles_per_seq - 1).astype(F32)
    x = x_ref[...]
    g = g_ref[...]
    h = _rms(x, g).astype(BF16)
    hp = _rms(xp_ref[...], g).astype(BF16)
    hn = _rms(xn_ref[...], g).astype(BF16)
    row = lax.broadcasted_iota(jnp.int32, (tm, 1), 0)

    def project(c):
        wg = wg_ref[c]
        gp = _dot(hp, wg)[SUBLANES - 1:SUBLANES, :] * keep_prev
        gn = _dot(hn, wg)[0:1, :] * keep_next
        return _dot(h, wg), gp, gn, _dot(h, wu_ref[c])

    def mix_down(c, gate, gp, gn, up):
        prev = jnp.where(row == 0, gp, pltpu.roll(gate, 1, axis=0))
        nxt = jnp.where(row == tm - 1, gn, pltpu.roll(gate, tm - 1, axis=0))
        cw = cw_ref[c]
        conv = cw[0:1, :] * prev + cw[1:2, :] * gate + cw[2:3, :] * nxt + cb_ref[c]
        return _dot((_gelu_tanh(conv) * up).astype(BF16), wd_ref[c])

    cur = project(0)
    for c in range(nchunk):
        ahead = project(c + 1) if c + 1 < nchunk else None
        down = mix_down(c, *cur)
        if c == 0:
            acc_ref[...] = down
        else:
            acc_ref[...] += down
        cur = ahead
    y = x + acc_ref[...]
    if final:
        y = _rms(y, fg_ref[...])
    out_ref[...] = y


def _ffn(x2, g, wg, wu, cw, cb, wd, final_g, seq, tm):
    n = x2.shape[0]
    nchunk = wg.shape[0]
    nb8 = n // SUBLANES
    step = tm // SUBLANES
    final = final_g is not None
    in_specs = [
        pl.BlockSpec((SUBLANES, D_MODEL), lambda i: (jnp.maximum(i * step - 1, 0), 0)),
        pl.BlockSpec((tm, D_MODEL), lambda i: (i, 0)),
        pl.BlockSpec((SUBLANES, D_MODEL), lambda i: (jnp.minimum((i + 1) * step, nb8 - 1), 0)),
        _resident(g.shape), _resident(wg.shape), _resident(wu.shape), _resident(cw.shape),
        _resident(cb.shape), _resident(wd.shape),
    ]
    args = [x2, x2, x2, g, wg, wu, cw, cb, wd]
    if final:
        in_specs.append(_resident(final_g.shape))
        args.append(final_g)
    return pl.pallas_call(
        functools.partial(_ffn_kernel, tm=tm, tiles_per_seq=seq // tm, nchunk=nchunk, final=final),
        grid=(n // tm,),
        in_specs=in_specs,
        out_specs=pl.BlockSpec((tm, D_MODEL), lambda i: (i, 0)),
        out_shape=jax.ShapeDtypeStruct((n, D_MODEL), F32),
        scratch_shapes=[pltpu.VMEM((tm, D_MODEL), F32)],
        compiler_params=_params("parallel"),
        name="ffn",
    )(*args)


def _proj_in_weight(w):
    a, cq, ckv, kr, c, dq, dk, dv = jnp.split(w, np.cumsum(
        [A_COLS, B_Q_RANK, B_KV_RANK, B_ROPE_DIM, C_COLS, D_Q_HEADS * HEAD_DIM, D_KV_HEADS * HEAD_DIM])[:], axis=1)
    half = B_ROPE_DIM // 2
    z = lambda width: jnp.zeros((D_MODEL, width), w.dtype)
    kr_blk = jnp.concatenate([z(B_NOPE_DIM), kr, z(LANES - B_NOPE_DIM - B_ROPE_DIM)], axis=1)
    krs_blk = jnp.concatenate([z(B_NOPE_DIM), kr[:, half:], kr[:, :half], z(LANES - B_NOPE_DIM - B_ROPE_DIM)], axis=1)
    dq = _permute_heads(dq, D_HEAD_ORDER, axis=1) * QK_SCALE
    a = a.reshape(D_MODEL, 3, N_DIL, BRANCH_W) * jnp.asarray([QK_SCALE, 1.0, 1.0], w.dtype)[None, :, None, None]
    a = jnp.transpose(a, (0, 2, 1, 3)).reshape(D_MODEL, N_DIL, QKV_COLS)
    c = jnp.concatenate([c[:, :BRANCH_W] * QK_SCALE, c[:, BRANCH_W:]], axis=1)
    dilated = [a[:, g] for g in range(1, N_DIL)]
    return jnp.concatenate([a[:, 0], c, dq, dk, dv] + dilated + [cq, ckv, kr_blk, krs_blk], axis=1).astype(BF16)


D_HEAD_ORDER = (0, 2, 1, 3)


def _permute_heads(w, order, axis):
    shape = w.shape
    split = shape[:axis] + (len(order), HEAD_DIM) + shape[axis + 1:]
    return jnp.take(w.reshape(split), np.asarray(order), axis=axis).reshape(shape)


def _ffn_chunks(w, axis, fc):
    if axis == 1:
        return jnp.transpose(w.reshape(w.shape[0], D_FF // fc, fc), (1, 0, 2))
    return w.reshape(D_FF // fc, fc, w.shape[1])


def _layer(x2, nb, seq, t5_table, d_bias, ct, st, p, final_g):
    n = x2.shape[0]
    tm = 512
    main, dilated, lat = _proj_in(x2, p["norm_mix_g"], p["w_in"], nb, seq, tm)
    main = main.reshape(nb, seq, MAIN_COLS)

    oa, sa = [], []
    for gi, (window, dil) in enumerate(DIL_CFG):
        radius = window // dil // 2
        bias = _band_bias(t5_table, range(gi * A_HEADS, (gi + 1) * A_HEADS), radius, dil)
        src = main if gi == 0 else dilated[gi - 1].reshape(nb * dil, seq // dil, QKV_COLS)
        o, s = _band_attention(bias, None, src, nstream=1, seq=seq // dil, cols=src.shape[-1],
                               qcol=0, kcol=1, vcol=2, kvw=BRANCH_W, radius=radius,
                               out_dtype=F32, emit_stats=True, name=f"band_a{gi}")
        if gi == 0:
            oa.append(o.reshape(n, BRANCH_W))
            sa.append(s.reshape(n, LANES))
        else:
            oa.append(o.reshape(nb, dil, seq // dil, BRANCH_W))
            sa.append(s.reshape(nb, dil, seq // dil, LANES))

    q, k, vt = _mla_prep(lat, ct, st, p["q_norm_g"], p["kv_norm_g"], *p["mla_w"], nb, seq, tm)
    hw = B_HEADS * MLA_HEAD_PAD
    yb = _mla_attn(q.reshape(nb, seq, hw), k.reshape(nb, seq // tm, tm, hw), vt, 512)

    yc = _na_attention(_na_bias(p["na_bias"]), main, seq, MAIN_C // BRANCH_W)

    yd = _band_attention(d_bias, p["sink_logit"], main, nstream=1, seq=seq, cols=MAIN_COLS,
                         qcol=MAIN_DQ // BRANCH_W, kcol=MAIN_DK // LANES, vcol=MAIN_DV // LANES, kvw=LANES,
                         radius=D_RADIUS, out_dtype=BF16, emit_stats=False, name="band_d")

    x2 = _merge(x2, p["norm_mix_g"], oa, sa, yb.reshape(n, BRANCH_W), yc.reshape(n, BRANCH_W),
                yd.reshape(n, BRANCH_W), p["w_gate"], p["w_branch"], p["w_out"], seq, tm)
    return _ffn(x2, p["norm_ffn_g"], p["w_ffn_gate"], p["w_ffn_up"], p["conv_w"], p["conv_b"],
                p["w_ffn_down"], final_g, seq, tm)


def kernel(x, t5_table, norm_mix_g, w_in, q_norm_g, w_uq, kv_norm_g, w_ukv, na_bias, sink_logit, w_gate, w_branch, w_out, norm_ffn_g, w_ffn_gate, w_ffn_up, conv_w, conv_b, w_ffn_down, final_g):
    nb, seq, _ = x.shape
    depth = w_in.shape[0]
    fc = 256
    x2 = x.reshape(nb * seq, D_MODEL)
    ct, st = _rope_tables(seq)
    d_bias = _band_bias(t5_table, [N_DIL * A_HEADS + h for h in D_HEAD_ORDER], D_RADIUS, 1)
    for layer in range(depth):
        wb = w_branch[layer]
        wb = jnp.concatenate([wb[:3], _permute_heads(wb[3], D_HEAD_ORDER, axis=0)[None]], axis=0)
        p = {
            "norm_mix_g": norm_mix_g[layer][None, :],
            "w_in": _proj_in_weight(w_in[layer]),
            "q_norm_g": q_norm_g[layer][None, :],
            "kv_norm_g": kv_norm_g[layer][None, :],
            "mla_w": _mla_weights(w_uq[layer], w_ukv[layer]),
            "na_bias": na_bias[layer],
            "sink_logit": sink_logit[layer][np.asarray(D_HEAD_ORDER)] * LOG2E,
            "w_gate": w_gate[layer].astype(BF16),
            "w_branch": wb.astype(BF16),
            "w_out": w_out[layer].astype(BF16),
            "norm_ffn_g": norm_ffn_g[layer][None, :],
            "w_ffn_gate": _ffn_chunks(w_ffn_gate[layer], 1, fc).astype(BF16),
            "w_ffn_up": _ffn_chunks(w_ffn_up[layer], 1, fc).astype(BF16),
            "conv_w": _ffn_chunks(conv_w[layer], 1, fc),
            "conv_b": conv_b[layer].reshape(D_FF // fc, 1, fc),
            "w_ffn_down": _ffn_chunks(w_ffn_down[layer], 0, fc).astype(BF16),
        }
        x2 = _layer(x2, nb, seq, t5_table, d_bias, ct, st, p,
                    final_g[None, :] if layer == depth - 1 else None)
    return x2.reshape(nb, seq, D_MODEL)
```

```python
import functools
import math

import jax
import jax.numpy as jnp
import numpy as np
from jax import lax
from jax.experimental import pallas as pl
from jax.experimental.pallas import tpu as pltpu

D_MODEL = 1024
GRID_W = 64
RMS_EPS = 1e-6
NEG_INF = -1e30
N_BRANCH = 4
BRANCH_W = 256

DIL_CFG = ((128, 1), (512, 4), (2048, 16))
N_DIL = 3
A_HEADS = 4
HEAD_DIM = 64
B_HEADS = 4
B_Q_RANK = 256
B_KV_RANK = 128
B_NOPE_DIM = 64
B_ROPE_DIM = 32
B_V_DIM = 64
ROPE_THETA = 10000.0
C_HEADS = 4
NA_ROWS = 8
NA_COLS = 16
D_Q_HEADS = 4
D_KV_HEADS = 2
D_RADIUS = 128
T5_BUCKETS = 32
T5_MAX_DIST = 1024
T5_HEADS = N_DIL * A_HEADS + D_Q_HEADS
D_FF = 2816
CONV_W = 3

A_COLS = 3 * N_DIL * A_HEADS * HEAD_DIM
C_COLS = 3 * C_HEADS * HEAD_DIM

LANES = 128
SUBLANES = 8
VMEM_LIMIT_BYTES = 56 * 1024 * 1024

QKV_COLS = 3 * BRANCH_W
MAIN_A0 = 0
MAIN_C = MAIN_A0 + QKV_COLS
MAIN_DQ = MAIN_C + QKV_COLS
MAIN_DK = MAIN_DQ + 256
MAIN_DV = MAIN_DK + 128
MAIN_COLS = MAIN_DV + 128
DIL_STREAMS = tuple(dil for _, dil in DIL_CFG[1:])
DIL_BASE = tuple(MAIN_COLS + i * QKV_COLS for i in range(len(DIL_STREAMS)))
LAT_BASE = MAIN_COLS + len(DIL_STREAMS) * QKV_COLS
LAT_CQ = 0
LAT_CKV = 256
LAT_KR = 384
LAT_KRS = 512
LAT_COLS = 640

MLA_HEAD_PAD = 128

BF16 = jnp.bfloat16
F32 = jnp.float32
LOG2E = math.log2(math.e)
QK_SCALE = HEAD_DIM ** -0.5
assert math.frexp(QK_SCALE)[0] == 0.5
MLA_LOGIT_SCALE = (B_NOPE_DIM + B_ROPE_DIM) ** -0.5 * LOG2E


def _rms(x, g):
    return x * lax.rsqrt(jnp.mean(x * x, axis=-1, keepdims=True) + RMS_EPS) * g


def _dot(a, b):
    return jnp.dot(a, b, preferred_element_type=F32)


def _dot_nt(a, b):
    return lax.dot_general(a, b, (((1,), (1,)), ((), ())), preferred_element_type=F32)


def _params(*sem):
    return pltpu.CompilerParams(dimension_semantics=sem, vmem_limit_bytes=VMEM_LIMIT_BYTES)


def _resident(shape):
    nd = len(shape)
    return pl.BlockSpec(shape, lambda *_: (0,) * nd)


def _proj_in_kernel(x_ref, g_ref, w_ref, ct_ref, st_ref, gq_ref, gkv_ref, wq1_ref, wq2_ref, wk_ref, wv_ref,
                    main_ref, *rest, tm, chunk):
    nd = len(DIL_STREAMS)
    dil_refs, (q_ref, k_ref, vt_ref), scr_refs = rest[:nd], rest[nd:nd + 3], rest[nd + 3:]
    h = _rms(x_ref[...], g_ref[...]).astype(BF16)
    for c0 in range(0, MAIN_COLS, chunk):
        main_ref[:, c0:c0 + chunk] = _dot(h, w_ref[:, c0:c0 + chunk]).astype(BF16)
    nslab = QKV_COLS // LANES
    for out_ref, scr_ref, base, dil in zip(dil_refs, scr_refs, DIL_BASE, DIL_STREAMS):
        res = _dot(h, w_ref[:, base:base + QKV_COLS])
        for k in range(nslab):
            scr_ref[k] = res[:, k * LANES:(k + 1) * LANES]
        for r in range(dil):
            rows = [scr_ref[k, pl.ds(r, tm // dil, stride=dil), :] for k in range(nslab)]
            out_ref[0, r] = jnp.concatenate(rows, axis=-1).astype(BF16)

    lat = _dot(h, w_ref[:, LAT_BASE:])
    ct1, st1 = ct_ref[...], st_ref[...]
    ct = jnp.concatenate([ct1] * B_HEADS, axis=-1)
    st = jnp.concatenate([st1] * B_HEADS, axis=-1)
    cn = _rms(lat[:, LAT_CQ:LAT_CQ + B_Q_RANK], gq_ref[...]).astype(BF16)
    q = _dot(cn, wq1_ref[...]) * ct + _dot(cn, wq2_ref[...]) * st
    q_ref[...] = (q * MLA_LOGIT_SCALE).astype(BF16)
    kvn = _rms(lat[:, LAT_CKV:LAT_CKV + B_KV_RANK], gkv_ref[...]).astype(BF16)
    krope = lat[:, LAT_KR:LAT_KR + LANES] * ct1 + lat[:, LAT_KRS:LAT_KRS + LANES] * st1
    k = _dot(kvn, wk_ref[...]) + jnp.concatenate([krope] * B_HEADS, axis=-1)
    k_ref[...] = k.astype(BF16)
    vt_ref[0, 0] = _dot(kvn, wv_ref[...]).T.astype(BF16)


def _proj_in(x2, g, w, ct, st, gq, gkv, wq1, wq2, wk, wv, nb, seq, tm):
    n = x2.shape[0]
    tps = seq // tm
    hw = B_HEADS * MLA_HEAD_PAD
    vw = B_HEADS * B_V_DIM
    row = lambda width: pl.BlockSpec((tm, width), lambda i: (i, 0))
    tab = pl.BlockSpec((tm, LANES), lambda i: (i % tps, 0))
    dil_specs = [pl.BlockSpec((1, d, tm // d, QKV_COLS), lambda i: (i // tps, 0, i % tps, 0)) for d in DIL_STREAMS]
    dil_shapes = [jax.ShapeDtypeStruct((nb, d, seq // d, QKV_COLS), BF16) for d in DIL_STREAMS]
    small = [gq, gkv, wq1, wq2, wk, wv]
    outs = pl.pallas_call(
        functools.partial(_proj_in_kernel, tm=tm, chunk=512),
        grid=(n // tm,),
        in_specs=[row(D_MODEL), _resident((1, D_MODEL)), _resident(w.shape), tab, tab]
                 + [_resident(a.shape) for a in small],
        out_specs=[row(MAIN_COLS)] + dil_specs
                  + [row(hw), row(hw), pl.BlockSpec((1, 1, vw, tm), lambda i: (i // tps, i % tps, 0, 0))],
        out_shape=[jax.ShapeDtypeStruct((n, MAIN_COLS), BF16)] + dil_shapes
                  + [jax.ShapeDtypeStruct((n, hw), BF16)] * 2 + [jax.ShapeDtypeStruct((nb, tps, vw, tm), BF16)],
        scratch_shapes=[pltpu.VMEM((QKV_COLS // LANES, tm, LANES), F32) for _ in DIL_STREAMS],
        compiler_params=_params("parallel"),
        name="proj_in",
    )(x2, g, w, ct, st, *small)
    nd = len(DIL_STREAMS)
    return outs[0], outs[1:1 + nd], outs[1 + nd:]


def _pair_rows(qv, lo):
    zero = jnp.zeros_like(qv)
    return jnp.concatenate([jnp.where(lo, qv, zero), jnp.where(lo, zero, qv)], axis=0)


def _pair_softmax(s, sink):
    m = jnp.max(s, axis=-1, keepdims=True)
    if sink is not None:
        m = jnp.maximum(m, sink)
    p = jnp.exp2(s - m)
    l = jnp.sum(p, axis=-1, keepdims=True)
    if sink is not None:
        l = l + jnp.exp2(sink - m)
    return p, m, l


def _band_kernel(*refs, seq, tq, radius, npair, nkv, has_sink, emit_stats):
    refs = list(refs)
    bias_ref = refs.pop(0)
    sink_ref = refs.pop(0) if has_sink else None
    q_ref, k_ref, v_ref, o_ref = refs[:4]
    st_ref = refs[4] if emit_stats else None
    win = tq + 2 * radius
    nblk = seq // tq
    lane = lax.broadcasted_iota(jnp.int32, (tq, LANES), 1)
    lo = lane < HEAD_DIM
    top = lax.broadcasted_iota(jnp.int32, (2 * tq, 1), 0) < tq

    ub = min(nblk, 4)
    assert nblk % ub == 0
    units = [(u, j) for u in range(ub) for j in range(npair)]
    kv_lanes = [slice((j if nkv == npair else 0) * LANES, ((j if nkv == npair else 0) + 1) * LANES)
                for j in range(npair)]

    def body(step, carry):
        q0 = [pl.multiple_of((step * ub + u) * tq, tq) for u in range(ub)]
        ks = [pl.multiple_of(jnp.clip(q - radius, 0, seq - win), radius) for q in q0]
        variant = [jnp.where(step * ub + u == 0, 0, jnp.where(step * ub + u == nblk - 1, 2, 1)) for u in range(ub)]
        s = {}
        for u, j in units:
            q2 = _pair_rows(q_ref[0, pl.ds(q0[u], tq), j * LANES:(j + 1) * LANES], lo)
            s[u, j] = _dot_nt(q2, k_ref[0, pl.ds(ks[u], win), kv_lanes[j]]) * LOG2E + bias_ref[variant[u], j]
        soft = {}
        for u, j in units:
            sink = jnp.where(top, sink_ref[2 * j], sink_ref[2 * j + 1]) if has_sink else None
            soft[u, j] = _pair_softmax(s[u, j], sink)
        for u in range(ub):
            outs = []
            stats = jnp.zeros((tq, LANES), F32)
            for j in range(npair):
                p, m, l = soft[u, j]
                pv = _dot(p.astype(BF16), v_ref[0, pl.ds(ks[u], win), kv_lanes[j]]) / l
                outs.append(jnp.where(lo, pv[:tq], pv[tq:]))
                if emit_stats:
                    for half, rows in enumerate((slice(0, tq), slice(tq, 2 * tq))):
                        stats = jnp.where(lane == 2 * j + half, m[rows], stats)
                        stats = jnp.where(lane == 2 * npair + 2 * j + half, l[rows], stats)
            o_ref[0, pl.ds(q0[u], tq), :] = jnp.concatenate(outs, axis=-1).astype(o_ref.dtype)
            if emit_stats:
                st_ref[0, pl.ds(q0[u], tq), :] = stats
        return carry

    lax.fori_loop(0, nblk // ub, body, 0)


def _band_attention(bias, sink, src, *, nstream, seq, cols, qcol, kcol, vcol, kvw, radius,
                    out_dtype, emit_stats, name):
    nb = src.shape[0]
    npair = bias.shape[1]
    tq = bias.shape[2] // 2
    qw = npair * LANES
    assert seq % tq == 0 and seq >= tq + 2 * radius and tq >= radius and bias.shape[3] == tq + 2 * radius
    kernel = functools.partial(_band_kernel, seq=seq, tq=tq, radius=radius, npair=npair, nkv=kvw // LANES,
                               has_sink=sink is not None, emit_stats=emit_stats)
    in_specs = [_resident(bias.shape)]
    args = [bias]
    if sink is not None:
        in_specs.append(pl.BlockSpec(memory_space=pltpu.SMEM))
        args.append(sink)
    in_specs += [
        pl.BlockSpec((1, seq, qw), lambda b, r: (b, 0, r * (cols // qw) + qcol)),
        pl.BlockSpec((1, seq, kvw), lambda b, r: (b, 0, r * (cols // kvw) + kcol)),
        pl.BlockSpec((1, seq, kvw), lambda b, r: (b, 0, r * (cols // kvw) + vcol)),
    ]
    args += [src, src, src]
    out_specs = [pl.BlockSpec((1, seq, qw), lambda b, r: (b, 0, r))]
    out_shape = [jax.ShapeDtypeStruct((nb, seq, nstream * qw), out_dtype)]
    if emit_stats:
        out_specs.append(pl.BlockSpec((1, seq, LANES), lambda b, r: (b, 0, r)))
        out_shape.append(jax.ShapeDtypeStruct((nb, seq, nstream * LANES), F32))
    outs = pl.pallas_call(
        kernel,
        grid=(nb, nstream),
        in_specs=in_specs,
        out_specs=out_specs,
        out_shape=out_shape,
        compiler_params=_params("parallel", "parallel"),
        name=name,
    )(*args)
    return outs if emit_stats else outs[0]


def _t5_bucket(rel):
    half = T5_BUCKETS // 2
    exact = half // 2
    n = np.abs(rel)
    large = exact + (np.log(np.maximum(n, 1) / exact) / math.log(T5_MAX_DIST / exact)
                     * (half - exact)).astype(np.int32)
    large = np.minimum(large, half - 1)
    return (np.where(rel > 0, half, 0) + np.where(n < exact, n, large)).astype(np.int32)


def _band_bias(t5_table, heads, radius, dil, tq=128):
    win = tq + 2 * radius
    period = win + tq
    j = np.arange(period)
    offs = np.array([0, -radius, -2 * radius])
    rel = offs[:, None] + np.where(j < win, j, j - period)[None, :]
    valid = np.abs(rel) <= radius
    bucket = _t5_bucket(dil * np.clip(rel, -radius, radius))
    g = jnp.transpose(t5_table[:, np.asarray(heads)][bucket], (0, 2, 1)).astype(F32)
    g = jnp.where(valid[:, None, :], g, NEG_INF)
    flat = jnp.tile(g, (1, 1, tq))[:, :, :tq * (period - 1)]
    tiles = flat.reshape(3, len(heads), tq, period - 1)[:, :, :, :win]
    return tiles.reshape(3, len(heads) // 2, 2 * tq, win) * LOG2E


def _mla_attn_kernel(q_ref, k_ref, vt_ref, o_ref, *, nchunk, tq):
    heads = range(B_HEADS)
    q = [q_ref[0, :, h * MLA_HEAD_PAD:(h + 1) * MLA_HEAD_PAD] for h in heads]

    def scores(c):
        return [_dot_nt(k_ref[0, c, :, h * MLA_HEAD_PAD:(h + 1) * MLA_HEAD_PAD], q[h]) for h in heads]

    init = (jnp.full((1, tq), -jnp.inf, F32), jnp.zeros((1, tq), F32), jnp.zeros((B_V_DIM, tq), F32))
    carry = (init,) * B_HEADS
    s = scores(0)
    for c in range(nchunk):
        ahead = scores(c + 1) if c + 1 < nchunk else None
        stats = []
        for h in heads:
            m, l, _ = carry[h]
            m_new = jnp.maximum(m, jnp.max(s[h], axis=0, keepdims=True))
            alpha = jnp.exp2(m - m_new)
            p = jnp.exp2(s[h] - m_new)
            stats.append((m_new, alpha * l + jnp.sum(p, axis=0, keepdims=True), alpha, p.astype(BF16)))
        new = []
        for h in heads:
            m_new, l, alpha, p = stats[h]
            pv = _dot(vt_ref[0, c, h * B_V_DIM:(h + 1) * B_V_DIM, :], p)
            new.append((m_new, l, alpha * carry[h][2] + pv))
        carry = tuple(new)
        s = ahead
    out_t = jnp.concatenate([acc / l for _, l, acc in carry], axis=0)
    o_ref[0] = out_t.T.astype(o_ref.dtype)


def _mla_attn(q, k, vt, tq):
    nb, seq, hw = q.shape
    nchunk = k.shape[1]
    return pl.pallas_call(
        functools.partial(_mla_attn_kernel, nchunk=nchunk, tq=tq),
        grid=(nb, seq // tq),
        in_specs=[
            pl.BlockSpec((1, tq, hw), lambda b, i: (b, i, 0)),
            pl.BlockSpec((1,) + k.shape[1:], lambda b, i: (b, 0, 0, 0)),
            pl.BlockSpec((1,) + vt.shape[1:], lambda b, i: (b, 0, 0, 0)),
        ],
        out_specs=pl.BlockSpec((1, tq, BRANCH_W), lambda b, i: (b, i, 0)),
        out_shape=jax.ShapeDtypeStruct((nb, seq, BRANCH_W), BF16),
        compiler_params=_params("parallel", "parallel"),
        name="mla_attn",
    )(q, k, vt)


def _rope_tables(seq):
    half = B_ROPE_DIM // 2
    inv = ROPE_THETA ** (-jnp.arange(half, dtype=F32) / half)
    ang = jnp.arange(seq, dtype=F32)[:, None] * inv[None, :]
    cos, sin = jnp.cos(ang), jnp.sin(ang)
    pad = jnp.zeros((seq, MLA_HEAD_PAD - B_NOPE_DIM - B_ROPE_DIM), F32)
    ct = jnp.concatenate([jnp.ones((seq, B_NOPE_DIM), F32), cos, cos, pad], axis=-1)
    st = jnp.concatenate([jnp.zeros((seq, B_NOPE_DIM), F32), -sin, sin, pad], axis=-1)
    return ct, st


def _mla_weights(w_uq, w_ukv):
    half = B_ROPE_DIM // 2
    wq = w_uq.reshape(B_Q_RANK, B_HEADS, B_NOPE_DIM + B_ROPE_DIM)
    nope, r1, r2 = wq[..., :B_NOPE_DIM], wq[..., B_NOPE_DIM:B_NOPE_DIM + half], wq[..., B_NOPE_DIM + half:]
    zpad = jnp.zeros((B_Q_RANK, B_HEADS, MLA_HEAD_PAD - B_NOPE_DIM - B_ROPE_DIM), F32)
    wq1 = jnp.concatenate([nope, r1, r2, zpad], axis=-1).reshape(B_Q_RANK, -1)
    wq2 = jnp.concatenate([jnp.zeros_like(nope), r2, r1, zpad], axis=-1).reshape(B_Q_RANK, -1)
    wkv = w_ukv.reshape(B_KV_RANK, B_HEADS, B_NOPE_DIM + B_V_DIM)
    z64 = jnp.zeros((B_KV_RANK, B_HEADS, MLA_HEAD_PAD - B_NOPE_DIM), F32)
    wk = jnp.concatenate([wkv[..., :B_NOPE_DIM], z64], axis=-1).reshape(B_KV_RANK, -1)
    wv = wkv[..., B_NOPE_DIM:].reshape(B_KV_RANK, -1)
    return tuple(w.astype(BF16) for w in (wq1, wq2, wk, wv))


def _na_kernel(bias_ref, q_ref, k_ref, v_ref, o_ref, *, rows):
    kh = NA_ROWS
    win = kh * GRID_W
    tq = GRID_W
    lo = lax.broadcasted_iota(jnp.int32, (tq, LANES), 1) < HEAD_DIM

    ub = 4
    npair = C_HEADS // 2
    units = [(u, j) for u in range(ub) for j in range(npair)]
    lanes = [slice(j * LANES, (j + 1) * LANES) for j in range(npair)]

    def body(step, carry):
        r = [step * ub + u for u in range(ub)]
        q0 = [pl.multiple_of(ri * tq, tq) for ri in r]
        r0 = [jnp.clip(ri - kh // 2, 0, rows - kh) for ri in r]
        ks = [pl.multiple_of(x * GRID_W, GRID_W) for x in r0]
        s = {}
        for u, j in units:
            q2 = _pair_rows(q_ref[0, pl.ds(q0[u], tq), lanes[j]], lo)
            s[u, j] = _dot_nt(q2, k_ref[0, pl.ds(ks[u], win), lanes[j]]) * LOG2E + bias_ref[r[u] - r0[u], j]
        soft = {uj: _pair_softmax(s[uj], None) for uj in units}
        for u in range(ub):
            outs = []
            for j in range(npair):
                p, _, l = soft[u, j]
                pv = _dot(p.astype(BF16), v_ref[0, pl.ds(ks[u], win), lanes[j]]) / l
                outs.append(jnp.where(lo, pv[:tq], pv[tq:]))
            o_ref[0, pl.ds(q0[u], tq), :] = jnp.concatenate(outs, axis=-1).astype(o_ref.dtype)
        return carry

    assert rows % ub == 0
    lax.fori_loop(0, rows // ub, body, 0)


def _na_attention(bias, src, seq, qcol):
    nb = src.shape[0]
    rows = seq // GRID_W
    assert rows >= NA_ROWS
    spec = lambda c: pl.BlockSpec((1, seq, BRANCH_W), lambda b: (b, 0, c))
    return pl.pallas_call(
        functools.partial(_na_kernel, rows=rows),
        grid=(nb,),
        in_specs=[_resident(bias.shape), spec(qcol), spec(qcol + 1), spec(qcol + 2)],
        out_specs=pl.BlockSpec((1, seq, BRANCH_W), lambda b: (b, 0, 0)),
        out_shape=jax.ShapeDtypeStruct((nb, seq, BRANCH_W), BF16),
        compiler_params=_params("parallel"),
        name="na_attn",
    )(bias, src, src, src)


def _na_bias(rpb):
    kh = NA_ROWS
    qc = np.arange(GRID_W)[:, None]
    kc = np.arange(GRID_W)[None, :]
    start = np.clip(qc - NA_COLS // 2, 0, GRID_W - NA_COLS)
    col_ok = (kc >= start) & (kc < start + NA_COLS)
    dc = np.clip(kc - qc + NA_COLS - 1, 0, 2 * NA_COLS - 2)
    v = np.arange(kh)[:, None]
    dr = np.clip(np.arange(kh)[None, :] - v + NA_ROWS - 1, 0, 2 * NA_ROWS - 2)
    b = rpb[:, dr][..., dc]
    b = jnp.where(col_ok[None, None, None], b.astype(F32), NEG_INF)
    b = jnp.transpose(b, (1, 0, 3, 2, 4))
    return b.reshape(kh, rpb.shape[0] // 2, 2 * GRID_W, kh * GRID_W) * LOG2E


def _merge_kernel(x_ref, g_ref, oa0_ref, oa1_ref, oa2_ref, sa0_ref, sa1_ref, sa2_ref, yb_ref, yc_ref, yd_ref,
                  wg_ref, wb_ref, wo_ref, out_ref, *scr_refs, tm):
    x = x_ref[...]
    h = _rms(x, g_ref[...]).astype(BF16)

    def token_order(ref, scr_ref, dil):
        nslab = scr_ref.shape[0]
        for r in range(dil):
            blk = ref[0, r]
            for k in range(nslab):
                scr_ref[k, pl.ds(r, tm // dil, stride=dil), :] = blk[:, k * LANES:(k + 1) * LANES]
        return jnp.concatenate([scr_ref[k] for k in range(nslab)], axis=-1)

    o = [oa0_ref[...]] + [token_order(ref, scr, d) for ref, scr, d in zip((oa1_ref, oa2_ref), scr_refs[:2], DIL_STREAMS)]
    st = [sa0_ref[...]] + [token_order(ref, scr, d) for ref, scr, d in zip((sa1_ref, sa2_ref), scr_refs[2:], DIL_STREAMS)]
    m = [s[:, 0:A_HEADS] for s in st]
    l = [s[:, A_HEADS:2 * A_HEADS] for s in st]
    m_all = jnp.maximum(jnp.maximum(m[0], m[1]), m[2])
    wgt = [l[g] * jnp.exp2(m[g] - m_all) for g in range(N_DIL)]
    den = wgt[0] + wgt[1] + wgt[2]
    parts = []
    for hh in range(A_HEADS):
        hs = slice(hh * HEAD_DIM, (hh + 1) * HEAD_DIM)
        num = wgt[0][:, hh:hh + 1] * o[0][:, hs]
        for g in range(1, N_DIL):
            num = num + wgt[g][:, hh:hh + 1] * o[g][:, hs]
        parts.append(num / den[:, hh:hh + 1])
    ya = jnp.concatenate(parts, axis=-1).astype(BF16)

    branches = [ya, yb_ref[...], yc_ref[...], yd_ref[...]]
    logits = _dot(h, wg_ref[0])
    merged = None
    for i in range(N_BRANCH):
        ahead = _dot(h, wg_ref[i + 1]) if i + 1 < N_BRANCH else None
        term = jax.nn.sigmoid(logits) * _dot(branches[i], wb_ref[i])
        merged = term if merged is None else merged + term
        logits = ahead
    out_ref[...] = x + _dot(merged.astype(BF16), wo_ref[...])


def _merge(x2, g, oa, sa, yb, yc, yd, wg, wb, wo, seq, tm):
    n = x2.shape[0]
    tps = seq // tm
    row = lambda w: pl.BlockSpec((tm, w), lambda i: (i, 0))
    streams = lambda w: [pl.BlockSpec((1, d, tm // d, w), lambda i: (i // tps, 0, i % tps, 0)) for d in DIL_STREAMS]
    return pl.pallas_call(
        functools.partial(_merge_kernel, tm=tm),
        grid=(n // tm,),
        in_specs=[row(D_MODEL), _resident(g.shape), row(BRANCH_W)] + streams(BRANCH_W) + [row(LANES)]
                 + streams(LANES) + [row(BRANCH_W)] * 3
                 + [_resident(wg.shape), _resident(wb.shape), _resident(wo.shape)],
        out_specs=row(D_MODEL),
        out_shape=jax.ShapeDtypeStruct((n, D_MODEL), F32),
        scratch_shapes=[pltpu.VMEM((BRANCH_W // LANES, tm, LANES), F32) for _ in DIL_STREAMS]
                       + [pltpu.VMEM((1, tm, LANES), F32) for _ in DIL_STREAMS],
        compiler_params=_params("parallel"),
        name="merge",
    )(x2, g, *oa, *sa, yb, yc, yd, wg, wb, wo)


def _gelu_tanh(x):
    c = math.sqrt(2.0 / math.pi)
    half = 0.5 * x
    return half + half * jnp.tanh(x * (c + (c * 0.044715) * (x * x)))


def _ffn_kernel(*refs, tm, tiles_per_seq, fc, final):
    refs = list(refs)
    xp_ref, x_ref, xn_ref, g_ref, wg_ref, wu_ref, cw_ref, cb_ref, wd_ref = refs[:9]
    fg_ref = refs[9] if final else None
    out_ref, u_ref = refs[-2:]
    nchunk = D_FF // fc
    i = pl.program_id(0)
    t = i % tiles_per_seq
    keep_prev = (t != 0).astype(F32)
    keep_next = (t != tiles_per_seq - 1).astype(F32)
    x = x_ref[...]
    g = g_ref[...]
    h = _rms(x, g).astype(BF16)
    hp = _rms(xp_ref[...], g).astype(BF16)
    hn = _rms(xn_ref[...], g).astype(BF16)
    row = lax.broadcasted_iota(jnp.int32, (tm, 1), 0)

    def project(c):
        cs = slice(c * fc, (c + 1) * fc)
        wg = wg_ref[:, cs]
        gp = _dot(hp, wg)[SUBLANES - 1:SUBLANES, :] * keep_prev
        gn = _dot(hn, wg)[0:1, :] * keep_next
        return _dot(h, wg), gp, gn, _dot(h, wu_ref[:, cs])

    def mix(c, gate, gp, gn, up):
        cs = slice(c * fc, (c + 1) * fc)
        prev = jnp.where(row == 0, gp, pltpu.roll(gate, 1, axis=0))
        nxt = jnp.where(row == tm - 1, gn, pltpu.roll(gate, tm - 1, axis=0))
        conv = cw_ref[0:1, cs] * prev + cw_ref[1:2, cs] * gate + cw_ref[2:3, cs] * nxt + cb_ref[:, cs]
        u_ref[:, cs] = (_gelu_tanh(conv) * up).astype(BF16)

    cur = project(0)
    for c in range(nchunk):
        ahead = project(c + 1) if c + 1 < nchunk else None
        mix(c, *cur)
        cur = ahead
    y = x + _dot(u_ref[...], wd_ref[...])
    if final:
        y = _rms(y, fg_ref[...])
    out_ref[...] = y


def _ffn(x2, g, wg, wu, cw, cb, wd, final_g, seq, tm):
    n = x2.shape[0]
    fc = 256
    assert D_FF % fc == 0
    nb8 = n // SUBLANES
    step = tm // SUBLANES
    final = final_g is not None
    in_specs = [
        pl.BlockSpec((SUBLANES, D_MODEL), lambda i: (jnp.maximum(i * step - 1, 0), 0)),
        pl.BlockSpec((tm, D_MODEL), lambda i: (i, 0)),
        pl.BlockSpec((SUBLANES, D_MODEL), lambda i: (jnp.minimum((i + 1) * step, nb8 - 1), 0)),
        _resident(g.shape), _resident(wg.shape), _resident(wu.shape), _resident(cw.shape),
        _resident(cb.shape), _resident(wd.shape),
    ]
    args = [x2, x2, x2, g, wg, wu, cw, cb, wd]
    if final:
        in_specs.append(_resident(final_g.shape))
        args.append(final_g)
    return pl.pallas_call(
        functools.partial(_ffn_kernel, tm=tm, tiles_per_seq=seq // tm, fc=fc, final=final),
        grid=(n // tm,),
        in_specs=in_specs,
        out_specs=pl.BlockSpec((tm, D_MODEL), lambda i: (i, 0)),
        out_shape=jax.ShapeDtypeStruct((n, D_MODEL), F32),
        scratch_shapes=[pltpu.VMEM((tm, D_FF), BF16)],
        compiler_params=_params("parallel"),
        name="ffn",
    )(*args)


def _proj_in_weight(w):
    a, cq, ckv, kr, c, dq, dk, dv = jnp.split(w, np.cumsum(
        [A_COLS, B_Q_RANK, B_KV_RANK, B_ROPE_DIM, C_COLS, D_Q_HEADS * HEAD_DIM, D_KV_HEADS * HEAD_DIM])[:], axis=1)
    half = B_ROPE_DIM // 2
    z = lambda width: jnp.zeros((D_MODEL, width), w.dtype)
    kr_blk = jnp.concatenate([z(B_NOPE_DIM), kr, z(LANES - B_NOPE_DIM - B_ROPE_DIM)], axis=1)
    krs_blk = jnp.concatenate([z(B_NOPE_DIM), kr[:, half:], kr[:, :half], z(LANES - B_NOPE_DIM - B_ROPE_DIM)], axis=1)
    dq = _permute_heads(dq, D_HEAD_ORDER, axis=1) * QK_SCALE
    a = a.reshape(D_MODEL, 3, N_DIL, BRANCH_W) * jnp.asarray([QK_SCALE, 1.0, 1.0], w.dtype)[None, :, None, None]
    a = jnp.transpose(a, (0, 2, 1, 3)).reshape(D_MODEL, N_DIL, QKV_COLS)
    c = jnp.concatenate([c[:, :BRANCH_W] * QK_SCALE, c[:, BRANCH_W:]], axis=1)
    dilated = [a[:, g] for g in range(1, N_DIL)]
    return jnp.concatenate([a[:, 0], c, dq, dk, dv] + dilated + [cq, ckv, kr_blk, krs_blk], axis=1).astype(BF16)


D_HEAD_ORDER = (0, 2, 1, 3)


def _permute_heads(w, order, axis):
    shape = w.shape
    split = shape[:axis] + (len(order), HEAD_DIM) + shape[axis + 1:]
    return jnp.take(w.reshape(split), np.asarray(order), axis=axis).reshape(shape)


def _layer(x2, nb, seq, t5_table, d_bias, ct, st, p, final_g):
    n = x2.shape[0]
    tm = 512
    main, dilated, (q, k, vt) = _proj_in(x2, p["norm_mix_g"], p["w_in"], ct, st, p["q_norm_g"], p["kv_norm_g"],
                                         *p["mla_w"], nb, seq, tm)
    main = main.reshape(nb, seq, MAIN_COLS)

    oa, sa = [], []
    for gi, (window, dil) in enumerate(DIL_CFG):
        radius = window // dil // 2
        bias = _band_bias(t5_table, range(gi * A_HEADS, (gi + 1) * A_HEADS), radius, dil)
        src = main if gi == 0 else dilated[gi - 1].reshape(nb * dil, seq // dil, QKV_COLS)
        o, s = _band_attention(bias, None, src, nstream=1, seq=seq // dil, cols=src.shape[-1],
                               qcol=0, kcol=1, vcol=2, kvw=BRANCH_W, radius=radius,
                               out_dtype=F32, emit_stats=True, name=f"band_a{gi}")
        if gi == 0:
            oa.append(o.reshape(n, BRANCH_W))
            sa.append(s.reshape(n, LANES))
        else:
            oa.append(o.reshape(nb, dil, seq // dil, BRANCH_W))
            sa.append(s.reshape(nb, dil, seq // dil, LANES))

    hw =B_HEADS * MLA_HEAD_PAD
    yb = _mla_attn(q.reshape(nb, seq, hw), k.reshape(nb, seq // tm, tm, hw), vt, 512)

    yc = _na_attention(_na_bias(p["na_bias"]), main, seq, MAIN_C // BRANCH_W)

    yd = _band_attention(d_bias, p["sink_logit"], main, nstream=1, seq=seq, cols=MAIN_COLS,
                         qcol=MAIN_DQ // BRANCH_W, kcol=MAIN_DK // LANES, vcol=MAIN_DV // LANES, kvw=LANES,
                         radius=D_RADIUS, out_dtype=BF16, emit_stats=False, name="band_d")

    x2 = _merge(x2, p["norm_mix_g"], oa, sa, yb.reshape(n, BRANCH_W), yc.reshape(n, BRANCH_W),
                yd.reshape(n, BRANCH_W), p["w_gate"], p["w_branch"], p["w_out"], seq, tm)
    return _ffn(x2, p["norm_ffn_g"], p["w_ffn_gate"], p["w_ffn_up"], p["conv_w"], p["conv_b"],
                p["w_ffn_down"], final_g, seq, tm)


def kernel(x, t5_table, norm_mix_g, w_in, q_norm_g, w_uq, kv_norm_g, w_ukv, na_bias, sink_logit, w_gate, w_branch, w_out, norm_ffn_g, w_ffn_gate, w_ffn_up, conv_w, conv_b, w_ffn_down, final_g):
    nb, seq, _ = x.shape
    depth = w_in.shape[0]
    x2 =x.reshape(nb * seq, D_MODEL)
    ct, st = _rope_tables(seq)
    d_bias = _band_bias(t5_table, [N_DIL * A_HEADS + h for h in D_HEAD_ORDER], D_RADIUS, 1)
    for layer in range(depth):
        wb = w_branch[layer]
        wb = jnp.concatenate([wb[:3], _permute_heads(wb[3], D_HEAD_ORDER, axis=0)[None]], axis=0)
        p = {
            "norm_mix_g": norm_mix_g[layer][None, :],
            "w_in": _proj_in_weight(w_in[layer]),
            "q_norm_g": q_norm_g[layer][None, :],
            "kv_norm_g": kv_norm_g[layer][None, :],
            "mla_w": _mla_weights(w_uq[layer], w_ukv[layer]),
            "na_bias": na_bias[layer],
            "sink_logit": sink_logit[layer][np.asarray(D_HEAD_ORDER)] * LOG2E,
            "w_gate": w_gate[layer].astype(BF16),
            "w_branch": wb.astype(BF16),
            "w_out": w_out[layer].astype(BF16),
            "norm_ffn_g": norm_ffn_g[layer][None, :],
            "w_ffn_gate": w_ffn_gate[layer].astype(BF16),
            "w_ffn_up": w_ffn_up[layer].astype(BF16),
            "conv_w": conv_w[layer],
            "conv_b": conv_b[layer][None, :],
            "w_ffn_down": w_ffn_down[layer].astype(BF16),
        }
        x2 = _layer(x2, nb, seq, t5_table, d_bias, ct, st, p,
                    final_g[None, :] if layer == depth - 1 else None)
    return x2.reshape(nb, seq, D_MODEL)
```

```python
import functools
import math

import jax
import jax.numpy as jnp
import numpy as np
from jax import lax
from jax.experimental import pallas as pl
from jax.experimental.pallas import tpu as pltpu

D_MODEL = 1024
GRID_W = 64
RMS_EPS = 1e-6
NEG_INF = -1e30
N_BRANCH = 4
BRANCH_W = 256

DIL_CFG = ((128, 1), (512, 4), (2048, 16))
N_DIL = 3
A_HEADS = 4
HEAD_DIM = 64
B_HEADS = 4
B_Q_RANK = 256
B_KV_RANK = 128
B_NOPE_DIM = 64
B_ROPE_DIM = 32
B_V_DIM = 64
ROPE_THETA = 10000.0
C_HEADS = 4
NA_ROWS = 8
NA_COLS = 16
D_Q_HEADS = 4
D_KV_HEADS = 2
D_RADIUS = 128
T5_BUCKETS = 32
T5_MAX_DIST = 1024
T5_HEADS = N_DIL * A_HEADS + D_Q_HEADS
D_FF = 2816
CONV_W = 3

A_COLS = 3 * N_DIL * A_HEADS * HEAD_DIM
C_COLS = 3 * C_HEADS * HEAD_DIM

LANES = 128
SUBLANES = 8
VMEM_LIMIT_BYTES = 56 * 1024 * 1024

QKV_COLS = 3 * BRANCH_W
MAIN_A0 = 0
MAIN_C = MAIN_A0 + QKV_COLS
MAIN_DQ = MAIN_C + QKV_COLS
MAIN_DK = MAIN_DQ + 256
MAIN_DV = MAIN_DK + 128
MAIN_COLS = MAIN_DV + 128
DIL_STREAMS = tuple(dil for _, dil in DIL_CFG[1:])
DIL_BASE = tuple(MAIN_COLS + i * QKV_COLS for i in range(len(DIL_STREAMS)))
LAT_BASE = MAIN_COLS + len(DIL_STREAMS) * QKV_COLS
LAT_CQ = 0
LAT_CKV = 256
LAT_KR = 384
LAT_KRS = 512
LAT_COLS = 640

MLA_HEAD_PAD = 128

BF16 = jnp.bfloat16
F32 = jnp.float32
LOG2E = math.log2(math.e)
QK_SCALE = HEAD_DIM ** -0.5
assert math.frexp(QK_SCALE)[0] == 0.5
MLA_LOGIT_SCALE = (B_NOPE_DIM + B_ROPE_DIM) ** -0.5 * LOG2E


def _rms(x, g):
    return x * lax.rsqrt(jnp.mean(x * x, axis=-1, keepdims=True) + RMS_EPS) * g


def _dot(a, b):
    return jnp.dot(a, b, preferred_element_type=F32)


def _dot_nt(a, b):
    return lax.dot_general(a, b, (((1,), (1,)), ((), ())), preferred_element_type=F32)


def _params(*sem):
    return pltpu.CompilerParams(dimension_semantics=sem, vmem_limit_bytes=VMEM_LIMIT_BYTES)


def _resident(shape):
    nd = len(shape)
    return pl.BlockSpec(shape, lambda *_: (0,) * nd)


def _proj_in_kernel(x_ref, g_ref, w_ref, ct_ref, st_ref, gq_ref, gkv_ref, wq1_ref, wq2_ref, wk_ref, wv_ref,
                    main_ref, *rest, tm, chunk):
    nd = len(DIL_STREAMS)
    dil_refs, (q_ref, k_ref, vt_ref), scr_refs = rest[:nd], rest[nd:nd + 3], rest[nd + 3:]
    h = _rms(x_ref[...], g_ref[...]).astype(BF16)
    for c0 in range(0, MAIN_COLS, chunk):
        main_ref[:, c0:c0 + chunk] = _dot(h, w_ref[:, c0:c0 + chunk]).astype(BF16)
    nslab = QKV_COLS // LANES
    for out_ref, scr_ref, base, dil in zip(dil_refs, scr_refs, DIL_BASE, DIL_STREAMS):
        res = _dot(h, w_ref[:, base:base + QKV_COLS])
        for k in range(nslab):
            scr_ref[k] = res[:, k * LANES:(k + 1) * LANES]
        for r in range(dil):
            rows = [scr_ref[k, pl.ds(r, tm // dil, stride=dil), :] for k in range(nslab)]
            out_ref[0, r] = jnp.concatenate(rows, axis=-1).astype(BF16)

    lat = _dot(h, w_ref[:, LAT_BASE:])
    ct1, st1 = ct_ref[...], st_ref[...]
    ct = jnp.concatenate([ct1] * B_HEADS, axis=-1)
    st = jnp.concatenate([st1] * B_HEADS, axis=-1)
    cn = _rms(lat[:, LAT_CQ:LAT_CQ + B_Q_RANK], gq_ref[...]).astype(BF16)
    q = _dot(cn, wq1_ref[...]) * ct + _dot(cn, wq2_ref[...]) * st
    q_ref[...] = (q * MLA_LOGIT_SCALE).astype(BF16)
    kvn = _rms(lat[:, LAT_CKV:LAT_CKV + B_KV_RANK], gkv_ref[...]).astype(BF16)
    krope = lat[:, LAT_KR:LAT_KR + LANES] * ct1 + lat[:, LAT_KRS:LAT_KRS + LANES] * st1
    k = _dot(kvn, wk_ref[...]) + jnp.concatenate([krope] * B_HEADS, axis=-1)
    k_ref[...] = k.astype(BF16)
    vt_ref[0, 0] = _dot(kvn, wv_ref[...]).T.astype(BF16)


def _proj_in(x2, g, w, ct, st, gq, gkv, wq1, wq2, wk, wv, nb, seq, tm):
    n = x2.shape[0]
    tps = seq // tm
    hw = B_HEADS * MLA_HEAD_PAD
    vw = B_HEADS * B_V_DIM
    row = lambda width: pl.BlockSpec((tm, width), lambda i: (i, 0))
    tab = pl.BlockSpec((tm, LANES), lambda i: (i % tps, 0))
    dil_specs = [pl.BlockSpec((1, d, tm // d, QKV_COLS), lambda i: (i // tps, 0, i % tps, 0)) for d in DIL_STREAMS]
    dil_shapes = [jax.ShapeDtypeStruct((nb, d, seq // d, QKV_COLS), BF16) for d in DIL_STREAMS]
    small = [gq, gkv, wq1, wq2, wk, wv]
    outs = pl.pallas_call(
        functools.partial(_proj_in_kernel, tm=tm, chunk=512),
        grid=(n // tm,),
        in_specs=[row(D_MODEL), _resident((1, D_MODEL)), _resident(w.shape), tab, tab]
                 + [_resident(a.shape) for a in small],
        out_specs=[row(MAIN_COLS)] + dil_specs
                  + [row(hw), row(hw), pl.BlockSpec((1, 1, vw, tm), lambda i: (i // tps, i % tps, 0, 0))],
        out_shape=[jax.ShapeDtypeStruct((n, MAIN_COLS), BF16)] + dil_shapes
                  + [jax.ShapeDtypeStruct((n, hw), BF16)] * 2 + [jax.ShapeDtypeStruct((nb, tps, vw, tm), BF16)],
        scratch_shapes=[pltpu.VMEM((QKV_COLS // LANES, tm, LANES), F32) for _ in DIL_STREAMS],
        compiler_params=_params("parallel"),
        name="proj_in",
    )(x2, g, w, ct, st, *small)
    nd = len(DIL_STREAMS)
    return outs[0], outs[1:1 + nd], outs[1 + nd:]


def _pair_rows(qv, lo):
    zero = jnp.zeros_like(qv)
    return jnp.concatenate([jnp.where(lo, qv, zero), jnp.where(lo, zero, qv)], axis=0)


def _pair_softmax(s, sink):
    m = jnp.max(s, axis=-1, keepdims=True)
    if sink is not None:
        m = jnp.maximum(m, sink)
    p = jnp.exp2((s - m).astype(BF16))
    return p, m, (jnp.exp2(sink - m) if sink is not None else None)


def _pair_values(p, v, l_sink):
    res = _dot(p, jnp.concatenate([v, jnp.ones_like(v)], axis=-1))
    l = res[:, LANES:]
    if l_sink is not None:
        l = l + l_sink
    return res[:, :LANES] / l, l


def _band_kernel(*refs, seq, tq, radius, npair, nkv, has_sink, emit_stats):
    refs = list(refs)
    bias_ref = refs.pop(0)
    sink_ref = refs.pop(0) if has_sink else None
    q_ref, k_ref, v_ref, o_ref = refs[:4]
    st_ref = refs[4] if emit_stats else None
    win = tq + 2 * radius
    nblk = seq // tq
    lane = lax.broadcasted_iota(jnp.int32, (tq, LANES), 1)
    lo = lane < HEAD_DIM
    top = lax.broadcasted_iota(jnp.int32, (2 * tq, 1), 0) < tq

    nsl = q_ref.shape[0]
    ub = min(nsl * nblk, 4)
    assert (nsl * nblk) % ub == 0
    units = [(u, j) for u in range(ub) for j in range(npair)]
    kv_lanes = [slice((j if nkv == npair else 0) * LANES, ((j if nkv == npair else 0) + 1) * LANES)
                for j in range(npair)]

    def body(step, carry):
        flat = [step * ub + u for u in range(ub)]
        sl = [lax.div(f, nblk) for f in flat]
        blk = [lax.rem(f, nblk) for f in flat]
        q0 = [pl.multiple_of(b * tq, tq) for b in blk]
        ks = [pl.multiple_of(jnp.clip(q - radius, 0, seq - win), radius) for q in q0]
        variant = [jnp.where(b == 0, 0, jnp.where(b == nblk - 1, 2, 1)) for b in blk]
        s = {}
        for u, j in units:
            q2 = _pair_rows(q_ref[sl[u], pl.ds(q0[u], tq), j * LANES:(j + 1) * LANES], lo)
            s[u, j] = _dot_nt(q2, k_ref[sl[u], pl.ds(ks[u], win), kv_lanes[j]]) * LOG2E + bias_ref[variant[u], j]
        soft = {}
        for u, j in units:
            sink = jnp.where(top, sink_ref[2 * j], sink_ref[2 * j + 1]) if has_sink else None
            soft[u, j] = _pair_softmax(s[u, j], sink)
        for u in range(ub):
            outs = []
            stats = jnp.zeros((tq, LANES), F32)
            for j in range(npair):
                p, m, l_sink = soft[u, j]
                pv, l = _pair_values(p, v_ref[sl[u], pl.ds(ks[u], win), kv_lanes[j]], l_sink)
                outs.append(jnp.where(lo, pv[:tq], pv[tq:]))
                if emit_stats:
                    for half, rows in enumerate((slice(0, tq), slice(tq, 2 * tq))):
                        stats = jnp.where(lane == 2 * j + half, m[rows], stats)
                        stats = jnp.where(lane == 2 * npair + 2 * j + half, l[rows], stats)
            o_ref[sl[u], pl.ds(q0[u], tq), :] = jnp.concatenate(outs, axis=-1).astype(o_ref.dtype)
            if emit_stats:
                st_ref[sl[u], pl.ds(q0[u], tq), :] = stats
        return carry

    lax.fori_loop(0, nsl * nblk // ub, body, 0)


def _band_attention(bias, sink, src, *, nstream, seq, cols, qcol, kcol, vcol, kvw, radius,
                    out_dtype, emit_stats, name):
    nb = src.shape[0]
    npair = bias.shape[1]
    tq = bias.shape[2] // 2
    qw = npair * LANES
    assert seq % tq == 0 and seq >= tq + 2 * radius and tq >= radius and bias.shape[3] == tq + 2 * radius
    kernel = functools.partial(_band_kernel, seq=seq, tq=tq, radius=radius, npair=npair, nkv=kvw // LANES,
                               has_sink=sink is not None, emit_stats=emit_stats)
    in_specs = [_resident(bias.shape)]
    args = [bias]
    if sink is not None:
        in_specs.append(pl.BlockSpec(memory_space=pltpu.SMEM))
        args.append(sink)
    nsl = max(1, min(nb, 1024 // seq))
    assert nb % nsl == 0
    in_specs += [
        pl.BlockSpec((nsl, seq, qw), lambda b, r: (b, 0, r * (cols // qw) + qcol)),
        pl.BlockSpec((nsl, seq, kvw), lambda b, r: (b, 0, r * (cols // kvw) + kcol)),
        pl.BlockSpec((nsl, seq, kvw), lambda b, r: (b, 0, r * (cols // kvw) + vcol)),
    ]
    args += [src, src, src]
    out_specs = [pl.BlockSpec((nsl, seq, qw), lambda b, r: (b, 0, r))]
    out_shape = [jax.ShapeDtypeStruct((nb, seq, nstream * qw), out_dtype)]
    if emit_stats:
        out_specs.append(pl.BlockSpec((nsl, seq, LANES), lambda b, r: (b, 0, r)))
        out_shape.append(jax.ShapeDtypeStruct((nb, seq, nstream * LANES), F32))
    outs = pl.pallas_call(
        kernel,
        grid=(nb // nsl, nstream),
        in_specs=in_specs,
        out_specs=out_specs,
        out_shape=out_shape,
        compiler_params=_params("parallel", "parallel"),
        name=name,
    )(*args)
    return outs if emit_stats else outs[0]


def _t5_bucket(rel):
    half = T5_BUCKETS // 2
    exact = half // 2
    n = np.abs(rel)
    large = exact + (np.log(np.maximum(n, 1) / exact) / math.log(T5_MAX_DIST / exact)
                     * (half - exact)).astype(np.int32)
    large = np.minimum(large, half - 1)
    return (np.where(rel > 0, half, 0) + np.where(n < exact, n, large)).astype(np.int32)


def _band_bias(t5_table, heads, radius, dil, tq=128):
    win = tq + 2 * radius
    period = win + tq
    j = np.arange(period)
    offs = np.array([0, -radius, -2 * radius])
    rel = offs[:, None] + np.where(j < win, j, j - period)[None, :]
    valid = np.abs(rel) <= radius
    bucket = _t5_bucket(dil * np.clip(rel, -radius, radius))
    g = jnp.transpose(t5_table[:, np.asarray(heads)][bucket], (0, 2, 1)).astype(F32)
    g = jnp.where(valid[:, None, :], g, NEG_INF)
    flat = jnp.tile(g, (1, 1, tq))[:, :, :tq * (period - 1)]
    tiles = flat.reshape(3, len(heads), tq, period - 1)[:, :, :, :win]
    return tiles.reshape(3, len(heads) // 2, 2 * tq, win) * LOG2E


def _mla_attn_kernel(q_ref, k_ref, vt_ref, o_ref, *, nchunk, tq):
    heads = range(B_HEADS)
    q = [q_ref[0, :, h * MLA_HEAD_PAD:(h + 1) * MLA_HEAD_PAD] for h in heads]

    def scores(c):
        return [_dot_nt(k_ref[0, c, :, h * MLA_HEAD_PAD:(h + 1) * MLA_HEAD_PAD], q[h]) for h in heads]

    init = (jnp.full((1, tq), -jnp.inf, F32), jnp.zeros((1, tq), F32), jnp.zeros((B_V_DIM, tq), F32))
    carry = (init,) * B_HEADS
    s = scores(0)
    for c in range(nchunk):
        ahead = scores(c + 1) if c + 1 < nchunk else None
        stats = []
        for h in heads:
            m, l, _ = carry[h]
            m_new = jnp.maximum(m, jnp.max(s[h], axis=0, keepdims=True))
            alpha = jnp.exp2(m - m_new)
            p = jnp.exp2(s[h] - m_new)
            stats.append((m_new, alpha * l + jnp.sum(p, axis=0, keepdims=True), alpha, p.astype(BF16)))
        new = []
        for h in heads:
            m_new, l, alpha, p = stats[h]
            pv = _dot(vt_ref[0, c, h * B_V_DIM:(h + 1) * B_V_DIM, :], p)
            new.append((m_new, l, alpha * carry[h][2] + pv))
        carry = tuple(new)
        s = ahead
    out_t = jnp.concatenate([acc / l for _, l, acc in carry], axis=0)
    o_ref[0] = out_t.T.astype(o_ref.dtype)


def _mla_attn(q, k, vt, tq):
    nb, seq, hw = q.shape
    nchunk = k.shape[1]
    return pl.pallas_call(
        functools.partial(_mla_attn_kernel, nchunk=nchunk, tq=tq),
        grid=(nb, seq // tq),
        in_specs=[
            pl.BlockSpec((1, tq, hw), lambda b, i: (b, i, 0)),
            pl.BlockSpec((1,) + k.shape[1:], lambda b, i: (b, 0, 0, 0)),
            pl.BlockSpec((1,) + vt.shape[1:], lambda b, i: (b, 0, 0, 0)),
        ],
        out_specs=pl.BlockSpec((1, tq, BRANCH_W), lambda b, i: (b, i, 0)),
        out_shape=jax.ShapeDtypeStruct((nb, seq, BRANCH_W), BF16),
        compiler_params=_params("parallel", "parallel"),
        name="mla_attn",
    )(q, k, vt)


def _rope_tables(seq):
    half = B_ROPE_DIM // 2
    inv = ROPE_THETA ** (-jnp.arange(half, dtype=F32) / half)
    ang = jnp.arange(seq, dtype=F32)[:, None] * inv[None, :]
    cos, sin = jnp.cos(ang), jnp.sin(ang)
    pad = jnp.zeros((seq, MLA_HEAD_PAD - B_NOPE_DIM - B_ROPE_DIM), F32)
    ct = jnp.concatenate([jnp.ones((seq, B_NOPE_DIM), F32), cos, cos, pad], axis=-1)
    st = jnp.concatenate([jnp.zeros((seq, B_NOPE_DIM), F32), -sin, sin, pad], axis=-1)
    return ct, st


def _mla_weights(w_uq, w_ukv):
    half = B_ROPE_DIM // 2
    wq = w_uq.reshape(B_Q_RANK, B_HEADS, B_NOPE_DIM + B_ROPE_DIM)
    nope, r1, r2 = wq[..., :B_NOPE_DIM], wq[..., B_NOPE_DIM:B_NOPE_DIM + half], wq[..., B_NOPE_DIM + half:]
    zpad = jnp.zeros((B_Q_RANK, B_HEADS, MLA_HEAD_PAD - B_NOPE_DIM - B_ROPE_DIM), F32)
    wq1 = jnp.concatenate([nope, r1, r2, zpad], axis=-1).reshape(B_Q_RANK, -1)
    wq2 = jnp.concatenate([jnp.zeros_like(nope), r2, r1, zpad], axis=-1).reshape(B_Q_RANK, -1)
    wkv = w_ukv.reshape(B_KV_RANK, B_HEADS, B_NOPE_DIM + B_V_DIM)
    z64 = jnp.zeros((B_KV_RANK, B_HEADS, MLA_HEAD_PAD - B_NOPE_DIM), F32)
    wk = jnp.concatenate([wkv[..., :B_NOPE_DIM], z64], axis=-1).reshape(B_KV_RANK, -1)
    wv = wkv[..., B_NOPE_DIM:].reshape(B_KV_RANK, -1)
    return tuple(w.astype(BF16) for w in (wq1, wq2, wk, wv))


def _na_kernel(bias_ref, q_ref, k_ref, v_ref, o_ref, *, rows):
    kh = NA_ROWS
    win = kh * GRID_W
    tq = GRID_W
    lo = lax.broadcasted_iota(jnp.int32, (tq, LANES), 1) < HEAD_DIM

    ub = 4
    npair = C_HEADS // 2
    units = [(u, j) for u in range(ub) for j in range(npair)]
    lanes = [slice(j * LANES, (j + 1) * LANES) for j in range(npair)]

    def body(step, carry):
        r = [step * ub + u for u in range(ub)]
        q0 = [pl.multiple_of(ri * tq, tq) for ri in r]
        r0 = [jnp.clip(ri - kh // 2, 0, rows - kh) for ri in r]
        ks = [pl.multiple_of(x * GRID_W, GRID_W) for x in r0]
        s = {}
        for u, j in units:
            q2 = _pair_rows(q_ref[0, pl.ds(q0[u], tq), lanes[j]], lo)
            s[u, j] = _dot_nt(q2, k_ref[0, pl.ds(ks[u], win), lanes[j]]) * LOG2E + bias_ref[r[u] - r0[u], j]
        soft = {uj: _pair_softmax(s[uj], None) for uj in units}
        for u in range(ub):
            outs = []
            for j in range(npair):
                p, _, _ = soft[u, j]
                pv, _ = _pair_values(p, v_ref[0, pl.ds(ks[u], win), lanes[j]], None)
                outs.append(jnp.where(lo, pv[:tq], pv[tq:]))
            o_ref[0, pl.ds(q0[u], tq), :] = jnp.concatenate(outs, axis=-1).astype(o_ref.dtype)
        return carry

    assert rows % ub == 0
    lax.fori_loop(0, rows // ub, body, 0)


def _na_attention(bias, src, seq, qcol):
    nb = src.shape[0]
    rows = seq // GRID_W
    assert rows >= NA_ROWS
    spec = lambda c: pl.BlockSpec((1, seq, BRANCH_W), lambda b: (b, 0, c))
    return pl.pallas_call(
        functools.partial(_na_kernel, rows=rows),
        grid=(nb,),
        in_specs=[_resident(bias.shape), spec(qcol), spec(qcol + 1), spec(qcol + 2)],
        out_specs=pl.BlockSpec((1, seq, BRANCH_W), lambda b: (b, 0, 0)),
        out_shape=jax.ShapeDtypeStruct((nb, seq, BRANCH_W), BF16),
        compiler_params=_params("parallel"),
        name="na_attn",
    )(bias, src, src, src)


def _na_bias(rpb):
    kh = NA_ROWS
    qc = np.arange(GRID_W)[:, None]
    kc = np.arange(GRID_W)[None, :]
    start = np.clip(qc - NA_COLS // 2, 0, GRID_W - NA_COLS)
    col_ok = (kc >= start) & (kc < start + NA_COLS)
    dc = np.clip(kc - qc + NA_COLS - 1, 0, 2 * NA_COLS - 2)
    v = np.arange(kh)[:, None]
    dr = np.clip(np.arange(kh)[None, :] - v + NA_ROWS - 1, 0, 2 * NA_ROWS - 2)
    b = rpb[:, dr][..., dc]
    b = jnp.where(col_ok[None, None, None], b.astype(F32), NEG_INF)
    b = jnp.transpose(b, (1, 0, 3, 2, 4))
    return b.reshape(kh, rpb.shape[0] // 2, 2 * GRID_W, kh * GRID_W) * LOG2E


def _merge_kernel(x_ref, g_ref, oa0_ref, oa1_ref, oa2_ref, sa0_ref, sa1_ref, sa2_ref, yb_ref, yc_ref, yd_ref,
                  wg_ref, wb_ref, wo_ref, out_ref, *scr_refs, tm):
    x = x_ref[...]
    h = _rms(x, g_ref[...]).astype(BF16)

    def token_order(ref, scr_ref, dil):
        nslab = scr_ref.shape[0]
        for r in range(dil):
            blk = ref[0, r]
            for k in range(nslab):
                scr_ref[k, pl.ds(r, tm // dil, stride=dil), :] = blk[:, k * LANES:(k + 1) * LANES]
        return jnp.concatenate([scr_ref[k] for k in range(nslab)], axis=-1)

    o = [oa0_ref[...]] + [token_order(ref, scr, d) for ref, scr, d in zip((oa1_ref, oa2_ref), scr_refs[:2], DIL_STREAMS)]
    st = [sa0_ref[...]] + [token_order(ref, scr, d) for ref, scr, d in zip((sa1_ref, sa2_ref), scr_refs[2:], DIL_STREAMS)]
    m = [s[:, 0:A_HEADS] for s in st]
    l = [s[:, A_HEADS:2 * A_HEADS] for s in st]
    m_all = jnp.maximum(jnp.maximum(m[0], m[1]), m[2])
    wgt = [l[g] * jnp.exp2(m[g] - m_all) for g in range(N_DIL)]
    den = wgt[0] + wgt[1] + wgt[2]
    parts = []
    for hh in range(A_HEADS):
        hs = slice(hh * HEAD_DIM, (hh + 1) * HEAD_DIM)
        num = wgt[0][:, hh:hh + 1] * o[0][:, hs]
        for g in range(1, N_DIL):
            num = num + wgt[g][:, hh:hh + 1] * o[g][:, hs]
        parts.append(num / den[:, hh:hh + 1])
    ya = jnp.concatenate(parts, axis=-1).astype(BF16)

    branches = [ya, yb_ref[...], yc_ref[...], yd_ref[...]]
    order = (1, 2, 3, 0)
    logits = _dot(h, wg_ref[order[0]])
    merged = None
    for n, i in enumerate(order):
        ahead = _dot(h, wg_ref[order[n + 1]]) if n + 1 < N_BRANCH else None
        term = jax.nn.sigmoid(logits) * _dot(branches[i], wb_ref[i])
        merged = term if merged is None else merged + term
        logits = ahead
    out_ref[...] = x + _dot(merged.astype(BF16), wo_ref[...])


def _merge(x2, g, oa, sa, yb, yc, yd, wg, wb, wo, seq, tm):
    n = x2.shape[0]
    tps = seq // tm
    row = lambda w: pl.BlockSpec((tm, w), lambda i: (i, 0))
    streams = lambda w: [pl.BlockSpec((1, d, tm // d, w), lambda i: (i // tps, 0, i % tps, 0)) for d in DIL_STREAMS]
    return pl.pallas_call(
        functools.partial(_merge_kernel, tm=tm),
        grid=(n // tm,),
        in_specs=[row(D_MODEL), _resident(g.shape), row(BRANCH_W)] + streams(BRANCH_W) + [row(LANES)]
                 + streams(LANES) + [row(BRANCH_W)] * 3
                 + [_resident(wg.shape), _resident(wb.shape), _resident(wo.shape)],
        out_specs=row(D_MODEL),
        out_shape=jax.ShapeDtypeStruct((n, D_MODEL), F32),
        scratch_shapes=[pltpu.VMEM((BRANCH_W // LANES, tm, LANES), F32) for _ in DIL_STREAMS]
                       + [pltpu.VMEM((1, tm, LANES), F32) for _ in DIL_STREAMS],
        compiler_params=_params("parallel"),
        name="merge",
    )(x2, g, *oa, *sa, yb, yc, yd, wg, wb, wo)


def _gelu_tanh(x):
    c = math.sqrt(2.0 / math.pi)
    half = 0.5 * x
    return half + half * jnp.tanh(x * (c + (c * 0.044715) * (x * x)))


def _ffn_kernel(*refs, tm, tiles_per_seq, fc, final):
    refs = list(refs)
    xp_ref, x_ref, xn_ref, g_ref, wg_ref, wu_ref, cw_ref, cb_ref, wd_ref = refs[:9]
    fg_ref = refs[9] if final else None
    out_ref, u_ref = refs[-2:]
    nchunk = D_FF // fc
    i = pl.program_id(0)
    t = i % tiles_per_seq
    keep_prev = (t != 0).astype(F32)
    keep_next = (t != tiles_per_seq - 1).astype(F32)
    x = x_ref[...]
    g = g_ref[...]
    h = _rms(x, g).astype(BF16)
    hp = _rms(xp_ref[...], g).astype(BF16)
    hn = _rms(xn_ref[...], g).astype(BF16)
    row = lax.broadcasted_iota(jnp.int32, (tm, 1), 0)

    def project(c):
        cs = slice(c * fc, (c + 1) * fc)
        wg = wg_ref[:, cs]
        gp = _dot(hp, wg)[SUBLANES - 1:SUBLANES, :] * keep_prev
        gn = _dot(hn, wg)[0:1, :] * keep_next
        return _dot(h, wg), gp, gn, _dot(h, wu_ref[:, cs])

    def mix(c, gate, gp, gn, up):
        cs = slice(c * fc, (c + 1) * fc)
        prev = jnp.where(row == 0, gp, pltpu.roll(gate, 1, axis=0))
        nxt = jnp.where(row == tm - 1, gn, pltpu.roll(gate, tm - 1, axis=0))
        conv = cw_ref[0:1, cs] * prev + cw_ref[1:2, cs] * gate + cw_ref[2:3, cs] * nxt + cb_ref[:, cs]
        u_ref[:, cs] = (_gelu_tanh(conv) * up).astype(BF16)

    cur = project(0)
    for c in range(nchunk):
        ahead = project(c + 1) if c + 1 < nchunk else None
        mix(c, *cur)
        cur = ahead
    y = x + _dot(u_ref[...], wd_ref[...])
    if final:
        y = _rms(y, fg_ref[...])
    out_ref[...] = y


def _ffn(x2, g, wg, wu, cw, cb, wd, final_g, seq, tm):
    n = x2.shape[0]
    fc = 256
    assert D_FF % fc == 0
    nb8 = n // SUBLANES
    step = tm // SUBLANES
    final = final_g is not None
    in_specs = [
        pl.BlockSpec((SUBLANES, D_MODEL), lambda i: (jnp.maximum(i * step - 1, 0), 0)),
        pl.BlockSpec((tm, D_MODEL), lambda i: (i, 0)),
        pl.BlockSpec((SUBLANES, D_MODEL), lambda i: (jnp.minimum((i + 1) * step, nb8 - 1), 0)),
        _resident(g.shape), _resident(wg.shape), _resident(wu.shape), _resident(cw.shape),
        _resident(cb.shape), _resident(wd.shape),
    ]
    args = [x2, x2, x2, g, wg, wu, cw, cb, wd]
    if final:
        in_specs.append(_resident(final_g.shape))
        args.append(final_g)
    return pl.pallas_call(
        functools.partial(_ffn_kernel, tm=tm, tiles_per_seq=seq // tm, fc=fc, final=final),
        grid=(n // tm,),
        in_specs=in_specs,
        out_specs=pl.BlockSpec((tm, D_MODEL), lambda i: (i, 0)),
        out_shape=jax.ShapeDtypeStruct((n, D_MODEL), F32),
        scratch_shapes=[pltpu.VMEM((tm, D_FF), BF16)],
        compiler_params=_params("parallel"),
        name="ffn",
    )(*args)


def _proj_in_weight(w):
    a, cq, ckv, kr, c, dq, dk, dv = jnp.split(w, np.cumsum(
        [A_COLS, B_Q_RANK, B_KV_RANK, B_ROPE_DIM, C_COLS, D_Q_HEADS * HEAD_DIM, D_KV_HEADS * HEAD_DIM])[:], axis=1)
    half = B_ROPE_DIM // 2
    z = lambda width: jnp.zeros((D_MODEL, width), w.dtype)
    kr_blk = jnp.concatenate([z(B_NOPE_DIM), kr, z(LANES - B_NOPE_DIM - B_ROPE_DIM)], axis=1)
    krs_blk = jnp.concatenate([z(B_NOPE_DIM), kr[:, half:], kr[:, :half], z(LANES - B_NOPE_DIM - B_ROPE_DIM)], axis=1)
    dq = _permute_heads(dq, D_HEAD_ORDER, axis=1) * QK_SCALE
    a = a.reshape(D_MODEL, 3, N_DIL, BRANCH_W) * jnp.asarray([QK_SCALE, 1.0, 1.0], w.dtype)[None, :, None, None]
    a = jnp.transpose(a, (0, 2, 1, 3)).reshape(D_MODEL, N_DIL, QKV_COLS)
    c = jnp.concatenate([c[:, :BRANCH_W] * QK_SCALE, c[:, BRANCH_W:]], axis=1)
    dilated = [a[:, g] for g in range(1, N_DIL)]
    return jnp.concatenate([a[:, 0], c, dq, dk, dv] + dilated + [cq, ckv, kr_blk, krs_blk], axis=1).astype(BF16)


D_HEAD_ORDER = (0, 2, 1, 3)


def _permute_heads(w, order, axis):
    shape = w.shape
    split = shape[:axis] + (len(order), HEAD_DIM) + shape[axis + 1:]
    return jnp.take(w.reshape(split), np.asarray(order), axis=axis).reshape(shape)


def _layer(x2, nb, seq, t5_table, d_bias, ct, st, p, final_g):
    n = x2.shape[0]
    tm = 512
    main, dilated, (q, k, vt) = _proj_in(x2, p["norm_mix_g"], p["w_in"], ct, st, p["q_norm_g"], p["kv_norm_g"],
                                         *p["mla_w"], nb, seq, tm)
    main = main.reshape(nb, seq, MAIN_COLS)

    oa, sa = [], []
    for gi, (window, dil) in enumerate(DIL_CFG):
        radius = window // dil // 2
        bias = _band_bias(t5_table, range(gi * A_HEADS, (gi + 1) * A_HEADS), radius, dil)
        src = main if gi == 0 else dilated[gi - 1].reshape(nb * dil, seq // dil, QKV_COLS)
        o, s = _band_attention(bias, None, src, nstream=1, seq=seq // dil, cols=src.shape[-1],
                               qcol=0, kcol=1, vcol=2, kvw=BRANCH_W, radius=radius,
                               out_dtype=F32, emit_stats=True, name=f"band_a{gi}")
        if gi == 0:
            oa.append(o.reshape(n, BRANCH_W))
            sa.append(s.reshape(n, LANES))
        else:
            oa.append(o.reshape(nb, dil, seq // dil, BRANCH_W))
            sa.append(s.reshape(nb, dil, seq // dil, LANES))

    hw =B_HEADS * MLA_HEAD_PAD
    yb = _mla_attn(q.reshape(nb, seq, hw), k.reshape(nb, seq // tm, tm, hw), vt, 512)

    yc = _na_attention(_na_bias(p["na_bias"]), main, seq, MAIN_C // BRANCH_W)

    yd = _band_attention(d_bias, p["sink_logit"], main, nstream=1, seq=seq, cols=MAIN_COLS,
                         qcol=MAIN_DQ // BRANCH_W, kcol=MAIN_DK // LANES, vcol=MAIN_DV // LANES, kvw=LANES,
                         radius=D_RADIUS, out_dtype=BF16, emit_stats=False, name="band_d")

    x2 = _merge(x2, p["norm_mix_g"], oa, sa, yb.reshape(n, BRANCH_W), yc.reshape(n, BRANCH_W),
                yd.reshape(n, BRANCH_W), p["w_gate"], p["w_branch"], p["w_out"], seq, tm)
    return _ffn(x2, p["norm_ffn_g"], p["w_ffn_gate"], p["w_ffn_up"], p["conv_w"], p["conv_b"],
                p["w_ffn_down"], final_g, seq, tm)


def kernel(x, t5_table, norm_mix_g, w_in, q_norm_g, w_uq, kv_norm_g, w_ukv, na_bias, sink_logit, w_gate, w_branch, w_out, norm_ffn_g, w_ffn_gate, w_ffn_up, conv_w, conv_b, w_ffn_down, final_g):
    nb, seq, _ = x.shape
    depth = w_in.shape[0]
    x2 =x.reshape(nb * seq, D_MODEL)
    ct, st = _rope_tables(seq)
    d_bias = _band_bias(t5_table, [N_DIL * A_HEADS + h for h in D_HEAD_ORDER], D_RADIUS, 1)
    for layer in range(depth):
        wb = w_branch[layer]
        wb = jnp.concatenate([wb[:3], _permute_heads(wb[3], D_HEAD_ORDER, axis=0)[None]], axis=0)
        p = {
            "norm_mix_g": norm_mix_g[layer][None, :],
            "w_in": _proj_in_weight(w_in[layer]),
            "q_norm_g": q_norm_g[layer][None, :],
            "kv_norm_g": kv_norm_g[layer][None, :],
            "mla_w": _mla_weights(w_uq[layer], w_ukv[layer]),
            "na_bias": na_bias[layer],
            "sink_logit": sink_logit[layer][np.asarray(D_HEAD_ORDER)] * LOG2E,
            "w_gate": w_gate[layer].astype(BF16),
            "w_branch": wb.astype(BF16),
            "w_out": w_out[layer].astype(BF16),
            "norm_ffn_g": norm_ffn_g[layer][None, :],
            "w_ffn_gate": w_ffn_gate[layer].astype(BF16),
            "w_ffn_up": w_ffn_up[layer].astype(BF16),
            "conv_w": conv_w[layer],
            "conv_b": conv_b[layer][None, :],
            "w_ffn_down": w_ffn_down[layer].astype(BF16),
        }
        x2 = _layer(x2, nb, seq, t5_table, d_bias, ct, st, p,
                    final_g[None, :] if layer == depth - 1 else None)
    return x2.reshape(nb, seq, D_MODEL)
```

```python
import functools
import math

import jax
import jax.numpy as jnp
import numpy as np
from jax import lax
from jax.experimental import pallas as pl
from jax.experimental.pallas import tpu as pltpu

D_MODEL = 1024
GRID_W = 64
RMS_EPS = 1e-6
NEG_INF = -1e30
N_BRANCH = 4
BRANCH_W = 256

DIL_CFG = ((128, 1), (512, 4), (2048, 16))
N_DIL = 3
A_HEADS = 4
HEAD_DIM = 64
B_HEADS = 4
B_Q_RANK = 256
B_KV_RANK = 128
B_NOPE_DIM = 64
B_ROPE_DIM = 32
B_V_DIM = 64
ROPE_THETA = 10000.0
C_HEADS = 4
NA_ROWS = 8
NA_COLS = 16
D_Q_HEADS = 4
D_KV_HEADS = 2
D_RADIUS = 128
T5_BUCKETS = 32
T5_MAX_DIST = 1024
T5_HEADS = N_DIL * A_HEADS + D_Q_HEADS
D_FF = 2816
CONV_W = 3

A_COLS = 3 * N_DIL * A_HEADS * HEAD_DIM
C_COLS = 3 * C_HEADS * HEAD_DIM

LANES = 128
SUBLANES = 8
VMEM_LIMIT_BYTES = 56 * 1024 * 1024

QKV_COLS = 3 * BRANCH_W
MAIN_A0 = 0
MAIN_C = MAIN_A0 + QKV_COLS
MAIN_DQ = MAIN_C + QKV_COLS
MAIN_DK = MAIN_DQ + 256
MAIN_DV = MAIN_DK + 128
MAIN_COLS = MAIN_DV + 128
DIL_STREAMS = tuple(dil for _, dil in DIL_CFG[1:])
DIL_BASE = tuple(MAIN_COLS + i * QKV_COLS for i in range(len(DIL_STREAMS)))
LAT_BASE = MAIN_COLS + len(DIL_STREAMS) * QKV_COLS
LAT_CQ = 0
LAT_CKV = 256
LAT_KR = 384
LAT_KRS = 512
LAT_COLS = 640

MLA_HEAD_PAD = 128

BF16 = jnp.bfloat16
F32 = jnp.float32
LOG2E = math.log2(math.e)
QK_SCALE = HEAD_DIM ** -0.5
assert math.frexp(QK_SCALE)[0] == 0.5
MLA_LOGIT_SCALE = (B_NOPE_DIM + B_ROPE_DIM) ** -0.5 * LOG2E


def _rms(x, g):
    return x * lax.rsqrt(jnp.mean(x * x, axis=-1, keepdims=True) + RMS_EPS) * g


def _dot(a, b):
    return jnp.dot(a, b, preferred_element_type=F32)


def _dot_nt(a, b):
    return lax.dot_general(a, b, (((1,), (1,)), ((), ())), preferred_element_type=F32)


def _params(*sem):
    return pltpu.CompilerParams(dimension_semantics=sem, vmem_limit_bytes=VMEM_LIMIT_BYTES)


def _resident(shape):
    nd = len(shape)
    return pl.BlockSpec(shape, lambda *_: (0,) * nd, pipeline_mode=pl.Buffered(1))


def _proj_in_kernel(x_ref, g_ref, w_ref, ct_ref, st_ref, gq_ref, gkv_ref, wq1_ref, wq2_ref, wk_ref, wv_ref,
                    main_ref, *rest, tm, chunk):
    nd = len(DIL_STREAMS)
    dil_refs, (q_ref, k_ref, vt_ref), scr_refs = rest[:nd], rest[nd:nd + 3], rest[nd + 3:]
    h = _rms(x_ref[...], g_ref[...]).astype(BF16)
    for c0 in range(0, MAIN_COLS, chunk):
        main_ref[:, c0:c0 + chunk] = _dot(h, w_ref[:, c0:c0 + chunk]).astype(BF16)
    nslab = QKV_COLS // LANES
    for out_ref, scr_ref, base, dil in zip(dil_refs, scr_refs, DIL_BASE, DIL_STREAMS):
        res = _dot(h, w_ref[:, base:base + QKV_COLS])
        for k in range(nslab):
            scr_ref[k] = res[:, k * LANES:(k + 1) * LANES]
        for r in range(dil):
            rows = [scr_ref[k, pl.ds(r, tm // dil, stride=dil), :] for k in range(nslab)]
            out_ref[0, r] = jnp.concatenate(rows, axis=-1).astype(BF16)

    lat = _dot(h, w_ref[:, LAT_BASE:])
    ct1, st1 = ct_ref[...], st_ref[...]
    ct = jnp.concatenate([ct1] * B_HEADS, axis=-1)
    st = jnp.concatenate([st1] * B_HEADS, axis=-1)
    cn = _rms(lat[:, LAT_CQ:LAT_CQ + B_Q_RANK], gq_ref[...]).astype(BF16)
    q = _dot(cn, wq1_ref[...]) * ct + _dot(cn, wq2_ref[...]) * st
    q_ref[...] = (q * MLA_LOGIT_SCALE).astype(BF16)
    kvn = _rms(lat[:, LAT_CKV:LAT_CKV + B_KV_RANK], gkv_ref[...]).astype(BF16)
    krope = lat[:, LAT_KR:LAT_KR + LANES] * ct1 + lat[:, LAT_KRS:LAT_KRS + LANES] * st1
    k = _dot(kvn, wk_ref[...]) + jnp.concatenate([krope] * B_HEADS, axis=-1)
    k_ref[...] = k.astype(BF16)
    vt_ref[0, 0] = _dot(kvn, wv_ref[...]).T.astype(BF16)


def _proj_in(x2, g, w, ct, st, gq, gkv, wq1, wq2, wk, wv, nb, seq, tm):
    n = x2.shape[0]
    tps = seq // tm
    hw = B_HEADS * MLA_HEAD_PAD
    vw = B_HEADS * B_V_DIM
    row = lambda width: pl.BlockSpec((tm, width), lambda i: (i, 0))
    tab = pl.BlockSpec((tm, LANES), lambda i: (i % tps, 0))
    dil_specs = [pl.BlockSpec((1, d, tm // d, QKV_COLS), lambda i: (i // tps, 0, i % tps, 0)) for d in DIL_STREAMS]
    dil_shapes = [jax.ShapeDtypeStruct((nb, d, seq // d, QKV_COLS), BF16) for d in DIL_STREAMS]
    small = [gq, gkv, wq1, wq2, wk, wv]
    outs = pl.pallas_call(
        functools.partial(_proj_in_kernel, tm=tm, chunk=512),
        grid=(n // tm,),
        in_specs=[row(D_MODEL), _resident((1, D_MODEL)), _resident(w.shape), tab, tab]
                 + [_resident(a.shape) for a in small],
        out_specs=[row(MAIN_COLS)] + dil_specs
                  + [row(hw), row(hw), pl.BlockSpec((1, 1, vw, tm), lambda i: (i // tps, i % tps, 0, 0))],
        out_shape=[jax.ShapeDtypeStruct((n, MAIN_COLS), BF16)] + dil_shapes
                  + [jax.ShapeDtypeStruct((n, hw), BF16)] * 2 + [jax.ShapeDtypeStruct((nb, tps, vw, tm), BF16)],
        scratch_shapes=[pltpu.VMEM((QKV_COLS // LANES, tm, LANES), F32) for _ in DIL_STREAMS],
        compiler_params=_params("parallel"),
        name="proj_in",
    )(x2, g, w, ct, st, *small)
    nd = len(DIL_STREAMS)
    return outs[0], outs[1:1 + nd], outs[1 + nd:]


def _pair_rows(qv, lo):
    zero = jnp.zeros_like(qv)
    return jnp.concatenate([jnp.where(lo, qv, zero), jnp.where(lo, zero, qv)], axis=0)


def _pair_softmax(s, sink):
    m = jnp.max(s, axis=-1, keepdims=True)
    if sink is not None:
        m = jnp.maximum(m, sink)
    p = jnp.exp2((s - m).astype(BF16))
    return p, m, (jnp.exp2(sink - m) if sink is not None else None)


def _pair_values(p, v, l_sink):
    res = _dot(p, jnp.concatenate([v, jnp.ones_like(v)], axis=-1))
    l = res[:, LANES:]
    if l_sink is not None:
        l = l + l_sink
    return res[:, :LANES] / l, l


def _band_kernel(*refs, seq, tq, radius, npair, nkv, has_sink, emit_stats):
    refs = list(refs)
    bias_ref = refs.pop(0)
    sink_ref = refs.pop(0) if has_sink else None
    q_ref, k_ref, v_ref, o_ref = refs[:4]
    st_ref = refs[4] if emit_stats else None
    win = tq + 2 * radius
    nblk = seq // tq
    lane = lax.broadcasted_iota(jnp.int32, (tq, LANES), 1)
    lo = lane < HEAD_DIM
    top = lax.broadcasted_iota(jnp.int32, (2 * tq, 1), 0) < tq

    nsl = q_ref.shape[0]
    ub = min(nsl * nblk, 4)
    assert (nsl * nblk) % ub == 0
    units = [(u, j) for u in range(ub) for j in range(npair)]
    kv_lanes = [slice((j if nkv == npair else 0) * LANES, ((j if nkv == npair else 0) + 1) * LANES)
                for j in range(npair)]

    def body(step, carry):
        flat = [step * ub + u for u in range(ub)]
        sl = [lax.div(f, nblk) for f in flat]
        blk = [lax.rem(f, nblk) for f in flat]
        q0 = [pl.multiple_of(b * tq, tq) for b in blk]
        ks = [pl.multiple_of(jnp.clip(q - radius, 0, seq - win), radius) for q in q0]
        variant = [jnp.where(b == 0, 0, jnp.where(b == nblk - 1, 2, 1)) for b in blk]
        s = {}
        for u, j in units:
            q2 = _pair_rows(q_ref[sl[u], pl.ds(q0[u], tq), j * LANES:(j + 1) * LANES], lo)
            s[u, j] = _dot_nt(q2, k_ref[sl[u], pl.ds(ks[u], win), kv_lanes[j]]) * LOG2E + bias_ref[variant[u], j]
        soft = {}
        for u, j in units:
            sink = jnp.where(top, sink_ref[2 * j], sink_ref[2 * j + 1]) if has_sink else None
            soft[u, j] = _pair_softmax(s[u, j], sink)
        for u in range(ub):
            outs = []
            stats = jnp.zeros((tq, LANES), F32)
            for j in range(npair):
                p, m, l_sink = soft[u, j]
                pv, l = _pair_values(p, v_ref[sl[u], pl.ds(ks[u], win), kv_lanes[j]], l_sink)
                outs.append(jnp.where(lo, pv[:tq], pv[tq:]))
                if emit_stats:
                    for half, rows in enumerate((slice(0, tq), slice(tq, 2 * tq))):
                        stats = jnp.where(lane == 2 * j + half, m[rows], stats)
                        stats = jnp.where(lane == 2 * npair + 2 * j + half, l[rows], stats)
            o_ref[sl[u], pl.ds(q0[u], tq), :] = jnp.concatenate(outs, axis=-1).astype(o_ref.dtype)
            if emit_stats:
                st_ref[sl[u], pl.ds(q0[u], tq), :] = stats
        return carry

    lax.fori_loop(0, nsl * nblk // ub, body, 0)


def _band_attention(bias, sink, src, *, nstream, seq, cols, qcol, kcol, vcol, kvw, radius,
                    out_dtype, emit_stats, name):
    nb = src.shape[0]
    npair = bias.shape[1]
    tq = bias.shape[2] // 2
    qw = npair * LANES
    assert seq % tq == 0 and seq >= tq + 2 * radius and tq >= radius and bias.shape[3] == tq + 2 * radius
    kernel = functools.partial(_band_kernel, seq=seq, tq=tq, radius=radius, npair=npair, nkv=kvw // LANES,
                               has_sink=sink is not None, emit_stats=emit_stats)
    in_specs = [_resident(bias.shape)]
    args = [bias]
    if sink is not None:
        in_specs.append(pl.BlockSpec(memory_space=pltpu.SMEM))
        args.append(sink)
    nsl = max(1, min(nb, 1024 // seq))
    assert nb % nsl == 0
    in_specs += [
        pl.BlockSpec((nsl, seq, qw), lambda b, r: (b, 0, r * (cols // qw) + qcol)),
        pl.BlockSpec((nsl, seq, kvw), lambda b, r: (b, 0, r * (cols // kvw) + kcol)),
        pl.BlockSpec((nsl, seq, kvw), lambda b, r: (b, 0, r * (cols // kvw) + vcol)),
    ]
    args += [src, src, src]
    out_specs = [pl.BlockSpec((nsl, seq, qw), lambda b, r: (b, 0, r))]
    out_shape = [jax.ShapeDtypeStruct((nb, seq, nstream * qw), out_dtype)]
    if emit_stats:
        out_specs.append(pl.BlockSpec((nsl, seq, LANES), lambda b, r: (b, 0, r)))
        out_shape.append(jax.ShapeDtypeStruct((nb, seq, nstream * LANES), F32))
    outs = pl.pallas_call(
        kernel,
        grid=(nb // nsl, nstream),
        in_specs=in_specs,
        out_specs=out_specs,
        out_shape=out_shape,
        compiler_params=_params("parallel", "parallel"),
        name=name,
    )(*args)
    return outs if emit_stats else outs[0]


def _t5_bucket(rel):
    half = T5_BUCKETS // 2
    exact = half // 2
    n = np.abs(rel)
    large = exact + (np.log(np.maximum(n, 1) / exact) / math.log(T5_MAX_DIST / exact)
                     * (half - exact)).astype(np.int32)
    large = np.minimum(large, half - 1)
    return (np.where(rel > 0, half, 0) + np.where(n < exact, n, large)).astype(np.int32)


def _band_bias(t5_table, heads, radius, dil, tq=128):
    win = tq + 2 * radius
    period = win + tq
    j = np.arange(period)
    offs = np.array([0, -radius, -2 * radius])
    rel = offs[:, None] + np.where(j < win, j, j - period)[None, :]
    valid = np.abs(rel) <= radius
    bucket = _t5_bucket(dil * np.clip(rel, -radius, radius))
    g = jnp.transpose(t5_table[:, np.asarray(heads)][bucket], (0, 2, 1)).astype(F32)
    g = jnp.where(valid[:, None, :], g, NEG_INF)
    flat = jnp.tile(g, (1, 1, tq))[:, :, :tq * (period - 1)]
    tiles = flat.reshape(3, len(heads), tq, period - 1)[:, :, :, :win]
    return tiles.reshape(3, len(heads) // 2, 2 * tq, win) * LOG2E


def _mla_attn_kernel(q_ref, k_ref, vt_ref, o_ref, *, nchunk, tq):
    heads = range(B_HEADS)
    q = [q_ref[0, :, h * MLA_HEAD_PAD:(h + 1) * MLA_HEAD_PAD] for h in heads]

    def scores(c):
        return [_dot_nt(k_ref[0, c, :, h * MLA_HEAD_PAD:(h + 1) * MLA_HEAD_PAD], q[h]) for h in heads]

    init = (jnp.full((1, tq), -jnp.inf, F32), jnp.zeros((1, tq), F32), jnp.zeros((B_V_DIM, tq), F32))
    carry = (init,) * B_HEADS
    s = scores(0)
    for c in range(nchunk):
        ahead = scores(c + 1) if c + 1 < nchunk else None
        stats = []
        for h in heads:
            m, l, _ = carry[h]
            m_new = jnp.maximum(m, jnp.max(s[h], axis=0, keepdims=True))
            alpha = jnp.exp2(m - m_new)
            p = jnp.exp2(s[h] - m_new)
            stats.append((m_new, alpha * l + jnp.sum(p, axis=0, keepdims=True), alpha, p.astype(BF16)))
        new = []
        for h in heads:
            m_new, l, alpha, p = stats[h]
            pv = _dot(vt_ref[0, c, h * B_V_DIM:(h + 1) * B_V_DIM, :], p)
            new.append((m_new, l, alpha * carry[h][2] + pv))
        carry = tuple(new)
        s = ahead
    out_t = jnp.concatenate([acc / l for _, l, acc in carry], axis=0)
    o_ref[0] = out_t.T.astype(o_ref.dtype)


def _mla_attn(q, k, vt, tq):
    nb, seq, hw = q.shape
    nchunk = k.shape[1]
    return pl.pallas_call(
        functools.partial(_mla_attn_kernel, nchunk=nchunk, tq=tq),
        grid=(nb, seq // tq),
        in_specs=[
            pl.BlockSpec((1, tq, hw), lambda b, i: (b, i, 0)),
            pl.BlockSpec((1,) + k.shape[1:], lambda b, i: (b, 0, 0, 0)),
            pl.BlockSpec((1,) + vt.shape[1:], lambda b, i: (b, 0, 0, 0)),
        ],
        out_specs=pl.BlockSpec((1, tq, BRANCH_W), lambda b, i: (b, i, 0)),
        out_shape=jax.ShapeDtypeStruct((nb, seq, BRANCH_W), BF16),
        compiler_params=_params("parallel", "parallel"),
        name="mla_attn",
    )(q, k, vt)


def _rope_tables(seq):
    half = B_ROPE_DIM // 2
    inv = ROPE_THETA ** (-jnp.arange(half, dtype=F32) / half)
    ang = jnp.arange(seq, dtype=F32)[:, None] * inv[None, :]
    cos, sin = jnp.cos(ang), jnp.sin(ang)
    pad = jnp.zeros((seq, MLA_HEAD_PAD - B_NOPE_DIM - B_ROPE_DIM), F32)
    ct = jnp.concatenate([jnp.ones((seq, B_NOPE_DIM), F32), cos, cos, pad], axis=-1)
    st = jnp.concatenate([jnp.zeros((seq, B_NOPE_DIM), F32), -sin, sin, pad], axis=-1)
    return ct, st


def _mla_weights(w_uq, w_ukv):
    half = B_ROPE_DIM // 2
    wq = w_uq.reshape(B_Q_RANK, B_HEADS, B_NOPE_DIM + B_ROPE_DIM)
    nope, r1, r2 = wq[..., :B_NOPE_DIM], wq[..., B_NOPE_DIM:B_NOPE_DIM + half], wq[..., B_NOPE_DIM + half:]
    zpad = jnp.zeros((B_Q_RANK, B_HEADS, MLA_HEAD_PAD - B_NOPE_DIM - B_ROPE_DIM), F32)
    wq1 = jnp.concatenate([nope, r1, r2, zpad], axis=-1).reshape(B_Q_RANK, -1)
    wq2 = jnp.concatenate([jnp.zeros_like(nope), r2, r1, zpad], axis=-1).reshape(B_Q_RANK, -1)
    wkv = w_ukv.reshape(B_KV_RANK, B_HEADS, B_NOPE_DIM + B_V_DIM)
    z64 = jnp.zeros((B_KV_RANK, B_HEADS, MLA_HEAD_PAD - B_NOPE_DIM), F32)
    wk = jnp.concatenate([wkv[..., :B_NOPE_DIM], z64], axis=-1).reshape(B_KV_RANK, -1)
    wv = wkv[..., B_NOPE_DIM:].reshape(B_KV_RANK, -1)
    return tuple(w.astype(BF16) for w in (wq1, wq2, wk, wv))


def _na_kernel(bias_ref, q_ref, k_ref, v_ref, o_ref, *, rows):
    kh = NA_ROWS
    win = kh * GRID_W
    tq = GRID_W
    lo = lax.broadcasted_iota(jnp.int32, (tq, LANES), 1) < HEAD_DIM

    ub = 4
    npair = C_HEADS // 2
    units = [(u, j) for u in range(ub) for j in range(npair)]
    lanes = [slice(j * LANES, (j + 1) * LANES) for j in range(npair)]

    def body(step, carry):
        r = [step * ub + u for u in range(ub)]
        q0 = [pl.multiple_of(ri * tq, tq) for ri in r]
        r0 = [jnp.clip(ri - kh // 2, 0, rows - kh) for ri in r]
        ks = [pl.multiple_of(x * GRID_W, GRID_W) for x in r0]
        s = {}
        for u, j in units:
            q2 = _pair_rows(q_ref[0, pl.ds(q0[u], tq), lanes[j]], lo)
            s[u, j] = _dot_nt(q2, k_ref[0, pl.ds(ks[u], win), lanes[j]]) * LOG2E + bias_ref[r[u] - r0[u], j]
        soft = {uj: _pair_softmax(s[uj], None) for uj in units}
        for u in range(ub):
            outs = []
            for j in range(npair):
                p, _, _ = soft[u, j]
                pv, _ = _pair_values(p, v_ref[0, pl.ds(ks[u], win), lanes[j]], None)
                outs.append(jnp.where(lo, pv[:tq], pv[tq:]))
            o_ref[0, pl.ds(q0[u], tq), :] = jnp.concatenate(outs, axis=-1).astype(o_ref.dtype)
        return carry

    assert rows % ub == 0
    lax.fori_loop(0, rows // ub, body, 0)


def _na_attention(bias, src, seq, qcol):
    nb = src.shape[0]
    rows = seq // GRID_W
    assert rows >= NA_ROWS
    spec = lambda c: pl.BlockSpec((1, seq, BRANCH_W), lambda b: (b, 0, c))
    return pl.pallas_call(
        functools.partial(_na_kernel, rows=rows),
        grid=(nb,),
        in_specs=[_resident(bias.shape), spec(qcol), spec(qcol + 1), spec(qcol + 2)],
        out_specs=pl.BlockSpec((1, seq, BRANCH_W), lambda b: (b, 0, 0)),
        out_shape=jax.ShapeDtypeStruct((nb, seq, BRANCH_W), BF16),
        compiler_params=_params("parallel"),
        name="na_attn",
    )(bias, src, src, src)


def _na_bias(rpb):
    kh = NA_ROWS
    qc = np.arange(GRID_W)[:, None]
    kc = np.arange(GRID_W)[None, :]
    start = np.clip(qc - NA_COLS // 2, 0, GRID_W - NA_COLS)
    col_ok = (kc >= start) & (kc < start + NA_COLS)
    dc = np.clip(kc - qc + NA_COLS - 1, 0, 2 * NA_COLS - 2)
    v = np.arange(kh)[:, None]
    dr = np.clip(np.arange(kh)[None, :] - v + NA_ROWS - 1, 0, 2 * NA_ROWS - 2)
    b = rpb[:, dr][..., dc]
    b = jnp.where(col_ok[None, None, None], b.astype(F32), NEG_INF)
    b = jnp.transpose(b, (1, 0, 3, 2, 4))
    return b.reshape(kh, rpb.shape[0] // 2, 2 * GRID_W, kh * GRID_W) * LOG2E


def _merge_kernel(x_ref, g_ref, oa0_ref, oa1_ref, oa2_ref, sa0_ref, sa1_ref, sa2_ref, yb_ref, yc_ref, yd_ref,
                  wg_ref, wb_ref, wo_ref, out_ref, *scr_refs, tm):
    x = x_ref[...]
    h = _rms(x, g_ref[...]).astype(BF16)

    def token_order(ref, scr_ref, dil):
        nslab = scr_ref.shape[0]
        for r in range(dil):
            blk = ref[0, r]
            for k in range(nslab):
                scr_ref[k, pl.ds(r, tm // dil, stride=dil), :] = blk[:, k * LANES:(k + 1) * LANES].astype(F32)
        return jnp.concatenate([scr_ref[k] for k in range(nslab)], axis=-1)

    o = [oa0_ref[...].astype(F32)] + [token_order(ref, scr, d)
                                      for ref, scr, d in zip((oa1_ref, oa2_ref), scr_refs[:2], DIL_STREAMS)]
    st = [sa0_ref[...]] + [token_order(ref, scr, d) for ref, scr, d in zip((sa1_ref, sa2_ref), scr_refs[2:], DIL_STREAMS)]
    m = [s[:, 0:A_HEADS] for s in st]
    l = [s[:, A_HEADS:2 * A_HEADS] for s in st]
    m_all = jnp.maximum(jnp.maximum(m[0], m[1]), m[2])
    wgt = [l[g] * jnp.exp2(m[g] - m_all) for g in range(N_DIL)]
    den = wgt[0] + wgt[1] + wgt[2]
    parts = []
    for hh in range(A_HEADS):
        hs = slice(hh * HEAD_DIM, (hh + 1) * HEAD_DIM)
        num = wgt[0][:, hh:hh + 1] * o[0][:, hs]
        for g in range(1, N_DIL):
            num = num + wgt[g][:, hh:hh + 1] * o[g][:, hs]
        parts.append(num / den[:, hh:hh + 1])
    ya = jnp.concatenate(parts, axis=-1).astype(BF16)

    branches = [ya, yb_ref[...], yc_ref[...], yd_ref[...]]
    order = (1, 2, 3, 0)
    logits = _dot(h, wg_ref[order[0]])
    merged = None
    for n, i in enumerate(order):
        ahead = _dot(h, wg_ref[order[n + 1]]) if n + 1 < N_BRANCH else None
        term = jax.nn.sigmoid(logits) * _dot(branches[i], wb_ref[i])
        merged = term if merged is None else merged + term
        logits = ahead
    out_ref[...] = x + _dot(merged.astype(BF16), wo_ref[...])


def _merge(x2, g, oa, sa, yb, yc, yd, wg, wb, wo, seq, tm):
    n = x2.shape[0]
    tps = seq // tm
    row = lambda w: pl.BlockSpec((tm, w), lambda i: (i, 0))
    streams = lambda w: [pl.BlockSpec((1, d, tm // d, w), lambda i: (i // tps, 0, i % tps, 0)) for d in DIL_STREAMS]
    return pl.pallas_call(
        functools.partial(_merge_kernel, tm=tm),
        grid=(n // tm,),
        in_specs=[row(D_MODEL), _resident(g.shape), row(BRANCH_W)] + streams(BRANCH_W) + [row(LANES)]
                 + streams(LANES) + [row(BRANCH_W)] * 3
                 + [_resident(wg.shape), _resident(wb.shape), _resident(wo.shape)],
        out_specs=row(D_MODEL),
        out_shape=jax.ShapeDtypeStruct((n, D_MODEL), F32),
        scratch_shapes=[pltpu.VMEM((BRANCH_W // LANES, tm, LANES), F32) for _ in DIL_STREAMS]
                       + [pltpu.VMEM((1, tm, LANES), F32) for _ in DIL_STREAMS],
        compiler_params=_params("parallel"),
        name="merge",
    )(x2, g, *oa, *sa, yb, yc, yd, wg, wb, wo)


def _gelu_tanh(x):
    c = math.sqrt(2.0 / math.pi)
    half = 0.5 * x
    return half + half * jnp.tanh(x * (c + (c * 0.044715) * (x * x)))


def _ffn_kernel(*refs, tm, tiles_per_seq, fc, final):
    refs = list(refs)
    xp_ref, x_ref, xn_ref, g_ref, wg_ref, wu_ref, cw_ref, cb_ref, wd_ref = refs[:9]
    fg_ref = refs[9] if final else None
    out_ref, u_ref = refs[-2:]
    nchunk = D_FF // fc
    i = pl.program_id(0)
    t = i % tiles_per_seq
    keep_prev = (t != 0).astype(F32)
    keep_next = (t != tiles_per_seq - 1).astype(F32)
    x = x_ref[...]
    g = g_ref[...]
    h = _rms(x, g).astype(BF16)
    hp = _rms(xp_ref[...], g).astype(BF16)
    hn = _rms(xn_ref[...], g).astype(BF16)
    row = lax.broadcasted_iota(jnp.int32, (tm, 1), 0)

    def project(c):
        cs = slice(c * fc, (c + 1) * fc)
        wg = wg_ref[:, cs]
        gp = _dot(hp, wg)[SUBLANES - 1:SUBLANES, :] * keep_prev
        gn = _dot(hn, wg)[0:1, :] * keep_next
        return _dot(h, wg), gp, gn, _dot(h, wu_ref[:, cs])

    def mix(c, gate, gp, gn, up):
        cs = slice(c * fc, (c + 1) * fc)
        prev = jnp.where(row == 0, gp, pltpu.roll(gate, 1, axis=0))
        nxt = jnp.where(row == tm - 1, gn, pltpu.roll(gate, tm - 1, axis=0))
        conv = cw_ref[0:1, cs] * prev + cw_ref[1:2, cs] * gate + cw_ref[2:3, cs] * nxt + cb_ref[:, cs]
        u_ref[:, cs] = (_gelu_tanh(conv) * up).astype(BF16)

    cur = project(0)
    for c in range(nchunk):
        ahead = project(c + 1) if c + 1 < nchunk else None
        mix(c, *cur)
        cur = ahead
    y = x + _dot(u_ref[...], wd_ref[...])
    if final:
        y = _rms(y, fg_ref[...])
    out_ref[...] = y


def _ffn(x2, g, wg, wu, cw, cb, wd, final_g, seq, tm):
    n = x2.shape[0]
    fc = 256
    assert D_FF % fc == 0
    nb8 = n // SUBLANES
    step = tm // SUBLANES
    final = final_g is not None
    in_specs = [
        pl.BlockSpec((SUBLANES, D_MODEL), lambda i: (jnp.maximum(i * step - 1, 0), 0)),
        pl.BlockSpec((tm, D_MODEL), lambda i: (i, 0)),
        pl.BlockSpec((SUBLANES, D_MODEL), lambda i: (jnp.minimum((i + 1) * step, nb8 - 1), 0)),
        _resident(g.shape), _resident(wg.shape), _resident(wu.shape), _resident(cw.shape),
        _resident(cb.shape), _resident(wd.shape),
    ]
    args = [x2, x2, x2, g, wg, wu, cw, cb, wd]
    if final:
        in_specs.append(_resident(final_g.shape))
        args.append(final_g)
    return pl.pallas_call(
        functools.partial(_ffn_kernel, tm=tm, tiles_per_seq=seq // tm, fc=fc, final=final),
        grid=(n // tm,),
        in_specs=in_specs,
        out_specs=pl.BlockSpec((tm, D_MODEL), lambda i: (i, 0)),
        out_shape=jax.ShapeDtypeStruct((n, D_MODEL), F32),
        scratch_shapes=[pltpu.VMEM((tm, D_FF), BF16)],
        compiler_params=_params("parallel"),
        name="ffn",
    )(*args)


def _proj_in_weight(w):
    a, cq, ckv, kr, c, dq, dk, dv = jnp.split(w, np.cumsum(
        [A_COLS, B_Q_RANK, B_KV_RANK, B_ROPE_DIM, C_COLS, D_Q_HEADS * HEAD_DIM, D_KV_HEADS * HEAD_DIM])[:], axis=1)
    half = B_ROPE_DIM // 2
    z = lambda width: jnp.zeros((D_MODEL, width), w.dtype)
    kr_blk = jnp.concatenate([z(B_NOPE_DIM), kr, z(LANES - B_NOPE_DIM - B_ROPE_DIM)], axis=1)
    krs_blk = jnp.concatenate([z(B_NOPE_DIM), kr[:, half:], kr[:, :half], z(LANES - B_NOPE_DIM - B_ROPE_DIM)], axis=1)
    dq = _permute_heads(dq, D_HEAD_ORDER, axis=1) * QK_SCALE
    a = a.reshape(D_MODEL, 3, N_DIL, BRANCH_W) * jnp.asarray([QK_SCALE, 1.0, 1.0], w.dtype)[None, :, None, None]
    a = jnp.transpose(a, (0, 2, 1, 3)).reshape(D_MODEL, N_DIL, QKV_COLS)
    c = jnp.concatenate([c[:, :BRANCH_W] * QK_SCALE, c[:, BRANCH_W:]], axis=1)
    dilated = [a[:, g] for g in range(1, N_DIL)]
    return jnp.concatenate([a[:, 0], c, dq, dk, dv] + dilated + [cq, ckv, kr_blk, krs_blk], axis=1).astype(BF16)


D_HEAD_ORDER = (0, 2, 1, 3)


def _permute_heads(w, order, axis):
    shape = w.shape
    split = shape[:axis] + (len(order), HEAD_DIM) + shape[axis + 1:]
    return jnp.take(w.reshape(split), np.asarray(order), axis=axis).reshape(shape)


def _layer(x2, nb, seq, t5_table, d_bias, ct, st, p, final_g):
    n = x2.shape[0]
    tm = 512
    main, dilated, (q, k, vt) = _proj_in(x2, p["norm_mix_g"], p["w_in"], ct, st, p["q_norm_g"], p["kv_norm_g"],
                                         *p["mla_w"], nb, seq, tm)
    main = main.reshape(nb, seq, MAIN_COLS)

    oa, sa = [], []
    for gi, (window, dil) in enumerate(DIL_CFG):
        radius = window // dil // 2
        bias = _band_bias(t5_table, range(gi * A_HEADS, (gi + 1) * A_HEADS), radius, dil)
        src = main if gi == 0 else dilated[gi - 1].reshape(nb * dil, seq // dil, QKV_COLS)
        o, s = _band_attention(bias, None, src, nstream=1, seq=seq // dil, cols=src.shape[-1],
                               qcol=0, kcol=1, vcol=2, kvw=BRANCH_W, radius=radius,
                               out_dtype=BF16, emit_stats=True, name=f"band_a{gi}")
        if gi == 0:
            oa.append(o.reshape(n, BRANCH_W))
            sa.append(s.reshape(n, LANES))
        else:
            oa.append(o.reshape(nb, dil, seq // dil, BRANCH_W))
            sa.append(s.reshape(nb, dil, seq // dil, LANES))

    hw =B_HEADS * MLA_HEAD_PAD
    yb = _mla_attn(q.reshape(nb, seq, hw), k.reshape(nb, seq // tm, tm, hw), vt, 512)

    yc = _na_attention(_na_bias(p["na_bias"]), main, seq, MAIN_C // BRANCH_W)

    yd = _band_attention(d_bias, p["sink_logit"], main, nstream=1, seq=seq, cols=MAIN_COLS,
                         qcol=MAIN_DQ // BRANCH_W, kcol=MAIN_DK // LANES, vcol=MAIN_DV // LANES, kvw=LANES,
                         radius=D_RADIUS, out_dtype=BF16, emit_stats=False, name="band_d")

    x2 = _merge(x2, p["norm_mix_g"], oa, sa, yb.reshape(n, BRANCH_W), yc.reshape(n, BRANCH_W),
                yd.reshape(n, BRANCH_W), p["w_gate"], p["w_branch"], p["w_out"], seq, tm)
    return _ffn(x2, p["norm_ffn_g"], p["w_ffn_gate"], p["w_ffn_up"], p["conv_w"], p["conv_b"],
                p["w_ffn_down"], final_g, seq, 2 * tm)


def kernel(x, t5_table, norm_mix_g, w_in, q_norm_g, w_uq, kv_norm_g, w_ukv, na_bias, sink_logit, w_gate, w_branch, w_out, norm_ffn_g, w_ffn_gate, w_ffn_up, conv_w, conv_b, w_ffn_down, final_g):
    nb, seq, _ = x.shape
    depth = w_in.shape[0]
    x2 =x.reshape(nb * seq, D_MODEL)
    ct, st = _rope_tables(seq)
    d_bias = _band_bias(t5_table, [N_DIL * A_HEADS + h for h in D_HEAD_ORDER], D_RADIUS, 1)
    for layer in range(depth):
        wb = w_branch[layer]
        wb = jnp.concatenate([wb[:3], _permute_heads(wb[3], D_HEAD_ORDER, axis=0)[None]], axis=0)
        p = {
            "norm_mix_g": norm_mix_g[layer][None, :],
            "w_in": _proj_in_weight(w_in[layer]),
            "q_norm_g": q_norm_g[layer][None, :],
            "kv_norm_g": kv_norm_g[layer][None, :],
            "mla_w": _mla_weights(w_uq[layer], w_ukv[layer]),
            "na_bias": na_bias[layer],
            "sink_logit": sink_logit[layer][np.asarray(D_HEAD_ORDER)] * LOG2E,
            "w_gate": w_gate[layer].astype(BF16),
            "w_branch": wb.astype(BF16),
            "w_out": w_out[layer].astype(BF16),
            "norm_ffn_g": norm_ffn_g[layer][None, :],
            "w_ffn_gate": w_ffn_gate[layer].astype(BF16),
            "w_ffn_up": w_ffn_up[layer].astype(BF16),
            "conv_w": conv_w[layer],
            "conv_b": conv_b[layer][None, :],
            "w_ffn_down": w_ffn_down[layer].astype(BF16),
        }
        x2 = _layer(x2, nb, seq, t5_table, d_bias, ct, st, p,
                    final_g[None, :] if layer == depth - 1 else None)
    return x2.reshape(nb, seq, D_MODEL)
```

```python
import functools
import math

import jax
import jax.numpy as jnp
import numpy as np
from jax import lax
from jax.experimental import pallas as pl
from jax.experimental.pallas import tpu as pltpu

D_MODEL = 1024
GRID_W = 64
RMS_EPS = 1e-6
NEG_INF = -1e30
N_BRANCH = 4
BRANCH_W = 256

DIL_CFG = ((128, 1), (512, 4), (2048, 16))
N_DIL = 3
A_HEADS = 4
HEAD_DIM = 64
B_HEADS = 4
B_Q_RANK = 256
B_KV_RANK = 128
B_NOPE_DIM = 64
B_ROPE_DIM = 32
B_V_DIM = 64
ROPE_THETA = 10000.0
C_HEADS = 4
NA_ROWS = 8
NA_COLS = 16
D_Q_HEADS = 4
D_KV_HEADS = 2
D_RADIUS = 128
T5_BUCKETS = 32
T5_MAX_DIST = 1024
T5_HEADS = N_DIL * A_HEADS + D_Q_HEADS
D_FF = 2816
CONV_W = 3

A_COLS = 3 * N_DIL * A_HEADS * HEAD_DIM
C_COLS = 3 * C_HEADS * HEAD_DIM

LANES = 128
SUBLANES = 8
VMEM_LIMIT_BYTES = 56 * 1024 * 1024

QKV_COLS = 3 * BRANCH_W
MAIN_A0 = 0
MAIN_C = MAIN_A0 + QKV_COLS
MAIN_DQ = MAIN_C + QKV_COLS
MAIN_DK = MAIN_DQ + 256
MAIN_DV = MAIN_DK + 128
MAIN_COLS = MAIN_DV + 128
DIL_STREAMS = tuple(dil for _, dil in DIL_CFG[1:])
DIL_BASE = tuple(MAIN_COLS + i * QKV_COLS for i in range(len(DIL_STREAMS)))
LAT_BASE = MAIN_COLS + len(DIL_STREAMS) * QKV_COLS
LAT_CQ = 0
LAT_CKV = 256
LAT_KR = 384
LAT_KRS = 512
LAT_COLS = 640

MLA_HEAD_PAD = 128

BF16 = jnp.bfloat16
F32 = jnp.float32
LOG2E = math.log2(math.e)
QK_SCALE = HEAD_DIM ** -0.5
assert math.frexp(QK_SCALE)[0] == 0.5
MLA_LOGIT_SCALE = (B_NOPE_DIM + B_ROPE_DIM) ** -0.5 * LOG2E


def _rms(x, g):
    return x * lax.rsqrt(jnp.mean(x * x, axis=-1, keepdims=True) + RMS_EPS) * g


def _dot(a, b):
    return jnp.dot(a, b, preferred_element_type=F32)


def _dot_nt(a, b):
    return lax.dot_general(a, b, (((1,), (1,)), ((), ())), preferred_element_type=F32)


def _params(*sem):
    return pltpu.CompilerParams(dimension_semantics=sem, vmem_limit_bytes=VMEM_LIMIT_BYTES)


def _resident(shape):
    nd = len(shape)
    return pl.BlockSpec(shape, lambda *_: (0,) * nd, pipeline_mode=pl.Buffered(1))


def _proj_in_kernel(x_ref, g_ref, w_ref, ct_ref, st_ref, gq_ref, gkv_ref, wq1_ref, wq2_ref, wk_ref, wv_ref,
                    main_ref, *rest, tm, chunk):
    nd = len(DIL_STREAMS)
    dil_refs, (q_ref, k_ref, vt_ref), scr_refs = rest[:nd], rest[nd:nd + 3], rest[nd + 3:]
    h = _rms(x_ref[...], g_ref[...]).astype(BF16)
    for c0 in range(0, MAIN_COLS, chunk):
        main_ref[:, c0:c0 + chunk] = _dot(h, w_ref[:, c0:c0 + chunk]).astype(BF16)
    nslab = QKV_COLS // LANES
    for out_ref, scr_ref, base, dil in zip(dil_refs, scr_refs, DIL_BASE, DIL_STREAMS):
        res = _dot(h, w_ref[:, base:base + QKV_COLS])
        for k in range(nslab):
            scr_ref[k] = res[:, k * LANES:(k + 1) * LANES]
        for r in range(dil):
            rows = [scr_ref[k, pl.ds(r, tm // dil, stride=dil), :] for k in range(nslab)]
            out_ref[0, r] = jnp.concatenate(rows, axis=-1).astype(BF16)

    lat = _dot(h, w_ref[:, LAT_BASE:])
    ct1, st1 = ct_ref[...], st_ref[...]
    ct = jnp.concatenate([ct1] * B_HEADS, axis=-1)
    st = jnp.concatenate([st1] * B_HEADS, axis=-1)
    cn = _rms(lat[:, LAT_CQ:LAT_CQ + B_Q_RANK], gq_ref[...]).astype(BF16)
    q = _dot(cn, wq1_ref[...]) * ct + _dot(cn, wq2_ref[...]) * st
    q_ref[...] = (q * MLA_LOGIT_SCALE).astype(BF16)
    kvn = _rms(lat[:, LAT_CKV:LAT_CKV + B_KV_RANK], gkv_ref[...]).astype(BF16)
    krope = lat[:, LAT_KR:LAT_KR + LANES] * ct1 + lat[:, LAT_KRS:LAT_KRS + LANES] * st1
    k = _dot(kvn, wk_ref[...]) + jnp.concatenate([krope] * B_HEADS, axis=-1)
    k_ref[...] = k.astype(BF16)
    vt_ref[0, 0] = _dot(kvn, wv_ref[...]).T.astype(BF16)


def _proj_in(x2, g, w, ct, st, gq, gkv, wq1, wq2, wk, wv, nb, seq, tm):
    n = x2.shape[0]
    tps = seq // tm
    hw = B_HEADS * MLA_HEAD_PAD
    vw = B_HEADS * B_V_DIM
    row = lambda width: pl.BlockSpec((tm, width), lambda i: (i, 0))
    tab = pl.BlockSpec((tm, LANES), lambda i: (i % tps, 0))
    dil_specs = [pl.BlockSpec((1, d, tm // d, QKV_COLS), lambda i: (i // tps, 0, i % tps, 0)) for d in DIL_STREAMS]
    dil_shapes = [jax.ShapeDtypeStruct((nb, d, seq // d, QKV_COLS), BF16) for d in DIL_STREAMS]
    small = [gq, gkv, wq1, wq2, wk, wv]
    outs = pl.pallas_call(
        functools.partial(_proj_in_kernel, tm=tm, chunk=512),
        grid=(n // tm,),
        in_specs=[row(D_MODEL), _resident((1, D_MODEL)), _resident(w.shape), tab, tab]
                 + [_resident(a.shape) for a in small],
        out_specs=[row(MAIN_COLS)] + dil_specs
                  + [row(hw), row(hw), pl.BlockSpec((1, 1, vw, tm), lambda i: (i // tps, i % tps, 0, 0))],
        out_shape=[jax.ShapeDtypeStruct((n, MAIN_COLS), BF16)] + dil_shapes
                  + [jax.ShapeDtypeStruct((n, hw), BF16)] * 2 + [jax.ShapeDtypeStruct((nb, tps, vw, tm), BF16)],
        scratch_shapes=[pltpu.VMEM((QKV_COLS // LANES, tm, LANES), F32) for _ in DIL_STREAMS],
        compiler_params=_params("parallel"),
        name="proj_in",
    )(x2, g, w, ct, st, *small)
    nd = len(DIL_STREAMS)
    return outs[0], outs[1:1 + nd], outs[1 + nd:]


def _pair_rows(qv, lo):
    zero = jnp.zeros_like(qv)
    return jnp.concatenate([jnp.where(lo, qv, zero), jnp.where(lo, zero, qv)], axis=0)


def _pair_softmax(s, sink):
    m = jnp.max(s, axis=-1, keepdims=True)
    if sink is not None:
        m = jnp.maximum(m, sink)
    p = jnp.exp2((s - m).astype(BF16))
    return p, m, (jnp.exp2(sink - m) if sink is not None else None)


def _pair_values(p, v, l_sink):
    res = _dot(p, jnp.concatenate([v, jnp.ones_like(v)], axis=-1))
    l = res[:, LANES:]
    if l_sink is not None:
        l = l + l_sink
    return res[:, :LANES] / l, l


def _band_kernel(*refs, seq, tq, radius, npair, nkv, has_sink, emit_stats):
    refs = list(refs)
    bias_ref = refs.pop(0)
    sink_ref = refs.pop(0) if has_sink else None
    q_ref, k_ref, v_ref, o_ref = refs[:4]
    st_ref = refs[4] if emit_stats else None
    win = tq + 2 * radius
    nblk = seq // tq
    lane = lax.broadcasted_iota(jnp.int32, (tq, LANES), 1)
    lo = lane < HEAD_DIM
    top = lax.broadcasted_iota(jnp.int32, (2 * tq, 1), 0) < tq

    nsl = q_ref.shape[0]
    ub = min(nsl * nblk, 4)
    assert (nsl * nblk) % ub == 0
    units = [(u, j) for u in range(ub) for j in range(npair)]
    kv_lanes = [slice((j if nkv == npair else 0) * LANES, ((j if nkv == npair else 0) + 1) * LANES)
                for j in range(npair)]

    def body(step, carry):
        flat = [step * ub + u for u in range(ub)]
        sl = [lax.div(f, nblk) for f in flat]
        blk = [lax.rem(f, nblk) for f in flat]
        q0 = [pl.multiple_of(b * tq, tq) for b in blk]
        ks = [pl.multiple_of(jnp.clip(q - radius, 0, seq - win), radius) for q in q0]
        variant = [jnp.where(b == 0, 0, jnp.where(b == nblk - 1, 2, 1)) for b in blk]
        s = {}
        for u, j in units:
            q2 = _pair_rows(q_ref[sl[u], pl.ds(q0[u], tq), j * LANES:(j + 1) * LANES], lo)
            s[u, j] = _dot_nt(q2, k_ref[sl[u], pl.ds(ks[u], win), kv_lanes[j]]) * LOG2E + bias_ref[variant[u], j]
        soft = {}
        for u, j in units:
            sink = jnp.where(top, sink_ref[2 * j], sink_ref[2 * j + 1]) if has_sink else None
            soft[u, j] = _pair_softmax(s[u, j], sink)
        for u in range(ub):
            outs = []
            stats = jnp.zeros((tq, LANES), F32)
            for j in range(npair):
                p, m, l_sink = soft[u, j]
                pv, l = _pair_values(p, v_ref[sl[u], pl.ds(ks[u], win), kv_lanes[j]], l_sink)
                outs.append(jnp.where(lo, pv[:tq], pv[tq:]))
                if emit_stats:
                    for half, rows in enumerate((slice(0, tq), slice(tq, 2 * tq))):
                        stats = jnp.where(lane == 2 * j + half, m[rows], stats)
                        stats = jnp.where(lane == 2 * npair + 2 * j + half, l[rows], stats)
            o_ref[sl[u], pl.ds(q0[u], tq), :] = jnp.concatenate(outs, axis=-1).astype(o_ref.dtype)
            if emit_stats:
                st_ref[sl[u], pl.ds(q0[u], tq), :] = stats
        return carry

    lax.fori_loop(0, nsl * nblk // ub, body, 0)


def _band_attention(bias, sink, src, *, nstream, seq, cols, qcol, kcol, vcol, kvw, radius,
                    out_dtype, emit_stats, name):
    nb = src.shape[0]
    npair = bias.shape[1]
    tq = bias.shape[2] // 2
    qw = npair * LANES
    assert seq % tq == 0 and seq >= tq + 2 * radius and tq >= radius and bias.shape[3] == tq + 2 * radius
    kernel = functools.partial(_band_kernel, seq=seq, tq=tq, radius=radius, npair=npair, nkv=kvw // LANES,
                               has_sink=sink is not None, emit_stats=emit_stats)
    in_specs = [_resident(bias.shape)]
    args = [bias]
    if sink is not None:
        in_specs.append(pl.BlockSpec(memory_space=pltpu.SMEM))
        args.append(sink)
    nsl = max(1, min(nb, 1024 // seq))
    assert nb % nsl == 0
    in_specs += [
        pl.BlockSpec((nsl, seq, qw), lambda b, r: (b, 0, r * (cols // qw) + qcol)),
        pl.BlockSpec((nsl, seq, kvw), lambda b, r: (b, 0, r * (cols // kvw) + kcol)),
        pl.BlockSpec((nsl, seq, kvw), lambda b, r: (b, 0, r * (cols // kvw) + vcol)),
    ]
    args += [src, src, src]
    out_specs = [pl.BlockSpec((nsl, seq, qw), lambda b, r: (b, 0, r))]
    out_shape = [jax.ShapeDtypeStruct((nb, seq, nstream * qw), out_dtype)]
    if emit_stats:
        out_specs.append(pl.BlockSpec((nsl, seq, LANES), lambda b, r: (b, 0, r)))
        out_shape.append(jax.ShapeDtypeStruct((nb, seq, nstream * LANES), F32))
    outs = pl.pallas_call(
        kernel,
        grid=(nb // nsl, nstream),
        in_specs=in_specs,
        out_specs=out_specs,
        out_shape=out_shape,
        compiler_params=_params("parallel", "parallel"),
        name=name,
    )(*args)
    return outs if emit_stats else outs[0]


def _t5_bucket(rel):
    half = T5_BUCKETS // 2
    exact = half // 2
    n = np.abs(rel)
    large = exact + (np.log(np.maximum(n, 1) / exact) / math.log(T5_MAX_DIST / exact)
                     * (half - exact)).astype(np.int32)
    large = np.minimum(large, half - 1)
    return (np.where(rel > 0, half, 0) + np.where(n < exact, n, large)).astype(np.int32)


def _band_bias(t5_table, heads, radius, dil, tq=128):
    win = tq + 2 * radius
    period = win + tq
    j = np.arange(period)
    offs = np.array([0, -radius, -2 * radius])
    rel = offs[:, None] + np.where(j < win, j, j - period)[None, :]
    valid = np.abs(rel) <= radius
    bucket = _t5_bucket(dil * np.clip(rel, -radius, radius))
    g = jnp.transpose(t5_table[:, np.asarray(heads)][bucket], (0, 2, 1)).astype(F32)
    g = jnp.where(valid[:, None, :], g, NEG_INF)
    flat = jnp.tile(g, (1, 1, tq))[:, :, :tq * (period - 1)]
    tiles = flat.reshape(3, len(heads), tq, period - 1)[:, :, :, :win]
    return tiles.reshape(3, len(heads) // 2, 2 * tq, win) * LOG2E


def _mla_attn_kernel(q_ref, k_ref, vt_ref, o_ref, *, nchunk, tq):
    heads = range(B_HEADS)
    q = [q_ref[0, :, h * MLA_HEAD_PAD:(h + 1) * MLA_HEAD_PAD] for h in heads]

    def scores(c):
        return [_dot_nt(k_ref[0, c, :, h * MLA_HEAD_PAD:(h + 1) * MLA_HEAD_PAD], q[h]) for h in heads]

    init = (jnp.full((1, tq), -jnp.inf, F32), jnp.zeros((1, tq), F32), jnp.zeros((B_V_DIM, tq), F32))
    carry = (init,) * B_HEADS
    s = scores(0)
    for c in range(nchunk):
        ahead = scores(c + 1) if c + 1 < nchunk else None
        stats = []
        for h in heads:
            m, l, _ = carry[h]
            m_new = jnp.maximum(m, jnp.max(s[h], axis=0, keepdims=True))
            alpha = jnp.exp2(m - m_new)
            p = jnp.exp2(s[h] - m_new)
            stats.append((m_new, alpha * l + jnp.sum(p, axis=0, keepdims=True), alpha, p.astype(BF16)))
        new = []
        for h in heads:
            m_new, l, alpha, p = stats[h]
            pv = _dot(vt_ref[0, c, h * B_V_DIM:(h + 1) * B_V_DIM, :], p)
            new.append((m_new, l, alpha * carry[h][2] + pv))
        carry = tuple(new)
        s = ahead
    out_t = jnp.concatenate([acc / l for _, l, acc in carry], axis=0)
    o_ref[0] = out_t.T.astype(o_ref.dtype)


def _mla_attn(q, k, vt, tq):
    nb, seq, hw = q.shape
    nchunk = k.shape[1]
    return pl.pallas_call(
        functools.partial(_mla_attn_kernel, nchunk=nchunk, tq=tq),
        grid=(nb, seq // tq),
        in_specs=[
            pl.BlockSpec((1, tq, hw), lambda b, i: (b, i, 0)),
            pl.BlockSpec((1,) + k.shape[1:], lambda b, i: (b, 0, 0, 0)),
            pl.BlockSpec((1,) + vt.shape[1:], lambda b, i: (b, 0, 0, 0)),
        ],
        out_specs=pl.BlockSpec((1, tq, BRANCH_W), lambda b, i: (b, i, 0)),
        out_shape=jax.ShapeDtypeStruct((nb, seq, BRANCH_W), BF16),
        compiler_params=_params("parallel", "parallel"),
        name="mla_attn",
    )(q, k, vt)


def _rope_tables(seq):
    half = B_ROPE_DIM // 2
    inv = ROPE_THETA ** (-jnp.arange(half, dtype=F32) / half)
    ang = jnp.arange(seq, dtype=F32)[:, None] * inv[None, :]
    cos, sin = jnp.cos(ang), jnp.sin(ang)
    pad = jnp.zeros((seq, MLA_HEAD_PAD - B_NOPE_DIM - B_ROPE_DIM), F32)
    ct = jnp.concatenate([jnp.ones((seq, B_NOPE_DIM), F32), cos, cos, pad], axis=-1)
    st = jnp.concatenate([jnp.zeros((seq, B_NOPE_DIM), F32), -sin, sin, pad], axis=-1)
    return ct, st


def _mla_weights(w_uq, w_ukv):
    half = B_ROPE_DIM // 2
    wq = w_uq.reshape(B_Q_RANK, B_HEADS, B_NOPE_DIM + B_ROPE_DIM)
    nope, r1, r2 = wq[..., :B_NOPE_DIM], wq[..., B_NOPE_DIM:B_NOPE_DIM + half], wq[..., B_NOPE_DIM + half:]
    zpad = jnp.zeros((B_Q_RANK, B_HEADS, MLA_HEAD_PAD - B_NOPE_DIM - B_ROPE_DIM), F32)
    wq1 = jnp.concatenate([nope, r1, r2, zpad], axis=-1).reshape(B_Q_RANK, -1)
    wq2 = jnp.concatenate([jnp.zeros_like(nope), r2, r1, zpad], axis=-1).reshape(B_Q_RANK, -1)
    wkv = w_ukv.reshape(B_KV_RANK, B_HEADS, B_NOPE_DIM + B_V_DIM)
    z64 = jnp.zeros((B_KV_RANK, B_HEADS, MLA_HEAD_PAD - B_NOPE_DIM), F32)
    wk = jnp.concatenate([wkv[..., :B_NOPE_DIM], z64], axis=-1).reshape(B_KV_RANK, -1)
    wv = wkv[..., B_NOPE_DIM:].reshape(B_KV_RANK, -1)
    return tuple(w.astype(BF16) for w in (wq1, wq2, wk, wv))


def _na_kernel(bias_ref, q_ref, k_ref, v_ref, o_ref, *, rows):
    kh = NA_ROWS
    win = kh * GRID_W
    tq = GRID_W
    lo = lax.broadcasted_iota(jnp.int32, (tq, LANES), 1) < HEAD_DIM

    ub = 4
    npair = C_HEADS // 2
    units = [(u, j) for u in range(ub) for j in range(npair)]
    lanes = [slice(j * LANES, (j + 1) * LANES) for j in range(npair)]

    def body(step, carry):
        r = [step * ub + u for u in range(ub)]
        q0 = [pl.multiple_of(ri * tq, tq) for ri in r]
        r0 = [jnp.clip(ri - kh // 2, 0, rows - kh) for ri in r]
        ks = [pl.multiple_of(x * GRID_W, GRID_W) for x in r0]
        s = {}
        for u, j in units:
            q2 = _pair_rows(q_ref[0, pl.ds(q0[u], tq), lanes[j]], lo)
            s[u, j] = _dot_nt(q2, k_ref[0, pl.ds(ks[u], win), lanes[j]]) * LOG2E + bias_ref[r[u] - r0[u], j]
        soft = {uj: _pair_softmax(s[uj], None) for uj in units}
        for u in range(ub):
            outs = []
            for j in range(npair):
                p, _, _ = soft[u, j]
                pv, _ = _pair_values(p, v_ref[0, pl.ds(ks[u], win), lanes[j]], None)
                outs.append(jnp.where(lo, pv[:tq], pv[tq:]))
            o_ref[0, pl.ds(q0[u], tq), :] = jnp.concatenate(outs, axis=-1).astype(o_ref.dtype)
        return carry

    assert rows % ub == 0
    lax.fori_loop(0, rows // ub, body, 0)


def _na_attention(bias, src, seq, qcol):
    nb = src.shape[0]
    rows = seq // GRID_W
    assert rows >= NA_ROWS
    spec = lambda c: pl.BlockSpec((1, seq, BRANCH_W), lambda b: (b, 0, c))
    return pl.pallas_call(
        functools.partial(_na_kernel, rows=rows),
        grid=(nb,),
        in_specs=[_resident(bias.shape), spec(qcol), spec(qcol + 1), spec(qcol + 2)],
        out_specs=pl.BlockSpec((1, seq, BRANCH_W), lambda b: (b, 0, 0)),
        out_shape=jax.ShapeDtypeStruct((nb, seq, BRANCH_W), BF16),
        compiler_params=_params("parallel"),
        name="na_attn",
    )(bias, src, src, src)


def _na_bias(rpb):
    kh = NA_ROWS
    qc = np.arange(GRID_W)[:, None]
    kc = np.arange(GRID_W)[None, :]
    start = np.clip(qc - NA_COLS // 2, 0, GRID_W - NA_COLS)
    col_ok = (kc >= start) & (kc < start + NA_COLS)
    heads = rpb.shape[0]
    period = 2 * GRID_W
    j = np.arange(period)
    d = np.where(j < GRID_W, j, j - period)
    g = jnp.where(np.abs(d) < NA_COLS, rpb[:, :, np.clip(d + NA_COLS - 1, 0, 2 * NA_COLS - 2)], 0.0)
    flat = jnp.tile(g.astype(F32), (1, 1, GRID_W))[:, :, :GRID_W * (period - 1)]
    tiles = flat.reshape(heads, 2 * NA_ROWS - 1, GRID_W, period - 1)[..., :GRID_W]
    b = jnp.stack([tiles[:, NA_ROWS - 1 - v:2 * NA_ROWS - 1 - v] for v in range(kh)])
    b = jnp.where(col_ok[None, None, None], b, NEG_INF)
    b = jnp.transpose(b, (0, 1, 3, 2, 4))
    return b.reshape(kh, heads // 2, 2 * GRID_W, kh * GRID_W) * LOG2E


def _merge_kernel(x_ref, g_ref, oa0_ref, oa1_ref, oa2_ref, sa0_ref, sa1_ref, sa2_ref, yb_ref, yc_ref, yd_ref,
                  wg_ref, wb_ref, wo_ref, out_ref, *scr_refs, tm):
    x = x_ref[...]
    h = _rms(x, g_ref[...]).astype(BF16)

    def token_order(ref, scr_ref, dil):
        nslab = scr_ref.shape[0]
        for r in range(dil):
            blk = ref[0, r]
            for k in range(nslab):
                scr_ref[k, pl.ds(r, tm // dil, stride=dil), :] = blk[:, k * LANES:(k + 1) * LANES].astype(F32)
        return jnp.concatenate([scr_ref[k] for k in range(nslab)], axis=-1)

    o = [oa0_ref[...].astype(F32)] + [token_order(ref, scr, d)
                                      for ref, scr, d in zip((oa1_ref, oa2_ref), scr_refs[:2], DIL_STREAMS)]
    st = [sa0_ref[...]] + [token_order(ref, scr, d) for ref, scr, d in zip((sa1_ref, sa2_ref), scr_refs[2:], DIL_STREAMS)]
    m = [s[:, 0:A_HEADS] for s in st]
    l = [s[:, A_HEADS:2 * A_HEADS] for s in st]
    m_all = jnp.maximum(jnp.maximum(m[0], m[1]), m[2])
    wgt = [l[g] * jnp.exp2(m[g] - m_all) for g in range(N_DIL)]
    den = wgt[0] + wgt[1] + wgt[2]
    parts = []
    for hh in range(A_HEADS):
        hs = slice(hh * HEAD_DIM, (hh + 1) * HEAD_DIM)
        num = wgt[0][:, hh:hh + 1] * o[0][:, hs]
        for g in range(1, N_DIL):
            num = num + wgt[g][:, hh:hh + 1] * o[g][:, hs]
        parts.append(num / den[:, hh:hh + 1])
    ya = jnp.concatenate(parts, axis=-1).astype(BF16)

    branches = [ya, yb_ref[...], yc_ref[...], yd_ref[...]]
    order = (1, 2, 3, 0)
    logits = _dot(h, wg_ref[order[0]])
    merged = None
    for n, i in enumerate(order):
        ahead = _dot(h, wg_ref[order[n + 1]]) if n + 1 < N_BRANCH else None
        term = jax.nn.sigmoid(logits) * _dot(branches[i], wb_ref[i])
        merged = term if merged is None else merged + term
        logits = ahead
    out_ref[...] = x + _dot(merged.astype(BF16), wo_ref[...])


def _merge(x2, g, oa, sa, yb, yc, yd, wg, wb, wo, seq, tm):
    n = x2.shape[0]
    tps = seq // tm
    row = lambda w: pl.BlockSpec((tm, w), lambda i: (i, 0))
    streams = lambda w: [pl.BlockSpec((1, d, tm // d, w), lambda i: (i // tps, 0, i % tps, 0)) for d in DIL_STREAMS]
    return pl.pallas_call(
        functools.partial(_merge_kernel, tm=tm),
        grid=(n // tm,),
        in_specs=[row(D_MODEL), _resident(g.shape), row(BRANCH_W)] + streams(BRANCH_W) + [row(LANES)]
                 + streams(LANES) + [row(BRANCH_W)] * 3
                 + [_resident(wg.shape), _resident(wb.shape), _resident(wo.shape)],
        out_specs=row(D_MODEL),
        out_shape=jax.ShapeDtypeStruct((n, D_MODEL), F32),
        scratch_shapes=[pltpu.VMEM((BRANCH_W // LANES, tm, LANES), F32) for _ in DIL_STREAMS]
                       + [pltpu.VMEM((1, tm, LANES), F32) for _ in DIL_STREAMS],
        compiler_params=_params("parallel"),
        name="merge",
    )(x2, g, *oa, *sa, yb, yc, yd, wg, wb, wo)


def _gelu_tanh(x):
    c = math.sqrt(2.0 / math.pi)
    half = 0.5 * x
    return half + half * jnp.tanh(x * (c + (c * 0.044715) * (x * x)))


def _ffn_kernel(*refs, tm, tiles_per_seq, fc, final):
    refs = list(refs)
    xp_ref, x_ref, xn_ref, g_ref, wg_ref, wu_ref, cw_ref, cb_ref, wd_ref = refs[:9]
    fg_ref = refs[9] if final else None
    out_ref, u_ref = refs[-2:]
    nchunk = D_FF // fc
    i = pl.program_id(0)
    t = i % tiles_per_seq
    keep_prev = (t != 0).astype(F32)
    keep_next = (t != tiles_per_seq - 1).astype(F32)
    x = x_ref[...]
    g = g_ref[...]
    h = _rms(x, g).astype(BF16)
    hp = _rms(xp_ref[...], g).astype(BF16)
    hn = _rms(xn_ref[...], g).astype(BF16)
    row = lax.broadcasted_iota(jnp.int32, (tm, 1), 0)

    def project(c):
        cs = slice(c * fc, (c + 1) * fc)
        wg = wg_ref[:, cs]
        gp = _dot(hp, wg)[SUBLANES - 1:SUBLANES, :] * keep_prev
        gn = _dot(hn, wg)[0:1, :] * keep_next
        return _dot(h, wg), gp, gn, _dot(h, wu_ref[:, cs])

    def mix(c, gate, gp, gn, up):
        cs = slice(c * fc, (c + 1) * fc)
        prev = jnp.where(row == 0, gp, pltpu.roll(gate, 1, axis=0))
        nxt = jnp.where(row == tm - 1, gn, pltpu.roll(gate, tm - 1, axis=0))
        conv = cw_ref[0:1, cs] * prev + cw_ref[1:2, cs] * gate + cw_ref[2:3, cs] * nxt + cb_ref[:, cs]
        u_ref[:, cs] = (_gelu_tanh(conv) * up).astype(BF16)

    cur = project(0)
    for c in range(nchunk):
        ahead = project(c + 1) if c + 1 < nchunk else None
        mix(c, *cur)
        cur = ahead
    y = x + _dot(u_ref[...], wd_ref[...])
    if final:
        y = _rms(y, fg_ref[...])
    out_ref[...] = y


def _ffn(x2, g, wg, wu, cw, cb, wd, final_g, seq, tm):
    n = x2.shape[0]
    fc = 256
    assert D_FF % fc == 0
    nb8 = n // SUBLANES
    step = tm // SUBLANES
    final = final_g is not None
    in_specs = [
        pl.BlockSpec((SUBLANES, D_MODEL), lambda i: (jnp.maximum(i * step - 1, 0), 0)),
        pl.BlockSpec((tm, D_MODEL), lambda i: (i, 0)),
        pl.BlockSpec((SUBLANES, D_MODEL), lambda i: (jnp.minimum((i + 1) * step, nb8 - 1), 0)),
        _resident(g.shape), _resident(wg.shape), _resident(wu.shape), _resident(cw.shape),
        _resident(cb.shape), _resident(wd.shape),
    ]
    args = [x2, x2, x2, g, wg, wu, cw, cb, wd]
    if final:
        in_specs.append(_resident(final_g.shape))
        args.append(final_g)
    return pl.pallas_call(
        functools.partial(_ffn_kernel, tm=tm, tiles_per_seq=seq // tm, fc=fc, final=final),
        grid=(n // tm,),
        in_specs=in_specs,
        out_specs=pl.BlockSpec((tm, D_MODEL), lambda i: (i, 0)),
        out_shape=jax.ShapeDtypeStruct((n, D_MODEL), F32),
        scratch_shapes=[pltpu.VMEM((tm, D_FF), BF16)],
        compiler_params=_params("parallel"),
        name="ffn",
    )(*args)


def _proj_in_weight(w):
    off = np.concatenate([[0], np.cumsum([A_COLS, B_Q_RANK, B_KV_RANK, B_ROPE_DIM, C_COLS,
                                          D_Q_HEADS * HEAD_DIM, D_KV_HEADS * HEAD_DIM])])
    a0, cq0, ckv0, kr0, c0, dq0, dk0, dv0 = (int(o) for o in off)
    half = B_ROPE_DIM // 2
    pad = LANES - B_NOPE_DIM - B_ROPE_DIM
    qkv = lambda base, step: [(base, BRANCH_W, QK_SCALE), (base + step, BRANCH_W, 1.0), (base + 2 * step, BRANCH_W, 1.0)]
    group = lambda g: qkv(a0 + g * BRANCH_W, N_DIL * BRANCH_W)
    segs = (group(0) + qkv(c0, BRANCH_W)
            + [(dq0 + h * HEAD_DIM, HEAD_DIM, QK_SCALE) for h in D_HEAD_ORDER]
            + [(dk0, D_KV_HEADS * HEAD_DIM, 1.0), (dv0, D_KV_HEADS * HEAD_DIM, 1.0)]
            + [s for g in range(1, N_DIL) for s in group(g)]
            + [(cq0, B_Q_RANK, 1.0), (ckv0, B_KV_RANK, 1.0)]
            + [(None, B_NOPE_DIM, 0.0), (kr0, B_ROPE_DIM, 1.0), (None, pad, 0.0)]
            + [(None, B_NOPE_DIM, 0.0), (kr0 + half, half, 1.0), (kr0, half, 1.0), (None, pad, 0.0)])
    cols = []
    for start, width, scale in segs:
        if start is None:
            cols.append(jnp.zeros((D_MODEL, width), w.dtype))
        else:
            piece = w[:, start:start + width]
            cols.append(piece if scale == 1.0 else piece * scale)
    out = jnp.concatenate(cols, axis=1).astype(BF16)
    assert out.shape[1] == LAT_BASE + LAT_COLS
    return out


D_HEAD_ORDER = (0, 2, 1, 3)


def _permute_heads(w, order, axis):
    return jnp.concatenate([lax.slice_in_dim(w, h * HEAD_DIM, (h + 1) * HEAD_DIM, axis=axis) for h in order],
                           axis=axis)


def _layer(x2, nb, seq, t5_table, d_bias, ct, st, p, final_g):
    n = x2.shape[0]
    tm = 512
    main, dilated, (q, k, vt) = _proj_in(x2, p["norm_mix_g"], p["w_in"], ct, st, p["q_norm_g"], p["kv_norm_g"],
                                         *p["mla_w"], nb, seq, tm)
    main = main.reshape(nb, seq, MAIN_COLS)

    oa, sa = [], []
    for gi, (window, dil) in enumerate(DIL_CFG):
        radius = window // dil // 2
        bias = _band_bias(t5_table, range(gi * A_HEADS, (gi + 1) * A_HEADS), radius, dil)
        src = main if gi == 0 else dilated[gi - 1].reshape(nb * dil, seq // dil, QKV_COLS)
        o, s = _band_attention(bias, None, src, nstream=1, seq=seq // dil, cols=src.shape[-1],
                               qcol=0, kcol=1, vcol=2, kvw=BRANCH_W, radius=radius,
                               out_dtype=BF16, emit_stats=True, name=f"band_a{gi}")
        if gi == 0:
            oa.append(o.reshape(n, BRANCH_W))
            sa.append(s.reshape(n, LANES))
        else:
            oa.append(o.reshape(nb, dil, seq // dil, BRANCH_W))
            sa.append(s.reshape(nb, dil, seq // dil, LANES))

    hw =B_HEADS * MLA_HEAD_PAD
    yb = _mla_attn(q.reshape(nb, seq, hw), k.reshape(nb, seq // tm, tm, hw), vt, 512)

    yc = _na_attention(_na_bias(p["na_bias"]), main, seq, MAIN_C // BRANCH_W)

    yd = _band_attention(d_bias, p["sink_logit"], main, nstream=1, seq=seq, cols=MAIN_COLS,
                         qcol=MAIN_DQ // BRANCH_W, kcol=MAIN_DK // LANES, vcol=MAIN_DV // LANES, kvw=LANES,
                         radius=D_RADIUS, out_dtype=BF16, emit_stats=False, name="band_d")

    x2 = _merge(x2, p["norm_mix_g"], oa, sa, yb.reshape(n, BRANCH_W), yc.reshape(n, BRANCH_W),
                yd.reshape(n, BRANCH_W), p["w_gate"], p["w_branch"], p["w_out"], seq, tm)
    return _ffn(x2, p["norm_ffn_g"], p["w_ffn_gate"], p["w_ffn_up"], p["conv_w"], p["conv_b"],
                p["w_ffn_down"], final_g, seq, 2 * tm)


def kernel(x, t5_table, norm_mix_g, w_in, q_norm_g, w_uq, kv_norm_g, w_ukv, na_bias, sink_logit, w_gate, w_branch, w_out, norm_ffn_g, w_ffn_gate, w_ffn_up, conv_w, conv_b, w_ffn_down, final_g):
    nb, seq, _ = x.shape
    depth = w_in.shape[0]
    x2 =x.reshape(nb * seq, D_MODEL)
    ct, st = _rope_tables(seq)
    d_bias = _band_bias(t5_table, [N_DIL * A_HEADS + h for h in D_HEAD_ORDER], D_RADIUS, 1)
    w_gate, w_out, w_ffn_gate, w_ffn_up, w_ffn_down = (
        w.astype(BF16) for w in (w_gate, w_out, w_ffn_gate, w_ffn_up, w_ffn_down))
    for layer in range(depth):
        wb = w_branch[layer]
        wb = jnp.concatenate([wb[:3], _permute_heads(wb[3], D_HEAD_ORDER, axis=0)[None]], axis=0)
        p = {
            "norm_mix_g": norm_mix_g[layer][None, :],
            "w_in": _proj_in_weight(w_in[layer]),
            "q_norm_g": q_norm_g[layer][None, :],
            "kv_norm_g": kv_norm_g[layer][None, :],
            "mla_w": _mla_weights(w_uq[layer], w_ukv[layer]),
            "na_bias": na_bias[layer],
            "sink_logit": sink_logit[layer][np.asarray(D_HEAD_ORDER)] * LOG2E,
            "w_gate": w_gate[layer],
            "w_branch": wb.astype(BF16),
            "w_out": w_out[layer],
            "norm_ffn_g": norm_ffn_g[layer][None, :],
            "w_ffn_gate": w_ffn_gate[layer],
            "w_ffn_up": w_ffn_up[layer],
            "conv_w": conv_w[layer],
            "conv_b": conv_b[layer][None, :],
            "w_ffn_down": w_ffn_down[layer],
        }
        x2 = _layer(x2, nb, seq, t5_table, d_bias, ct, st, p,
                    final_g[None, :] if layer == depth - 1 else None)
    return x2.reshape(nb, seq, D_MODEL)
```

```python
import functools
import math

import jax
import jax.numpy as jnp
import numpy as np
from jax import lax
from jax.experimental import pallas as pl
from jax.experimental.pallas import tpu as pltpu

D_MODEL = 1024
GRID_W = 64
RMS_EPS = 1e-6
NEG_INF = -1e30
N_BRANCH = 4
BRANCH_W = 256

DIL_CFG = ((128, 1), (512, 4), (2048, 16))
N_DIL = 3
A_HEADS = 4
HEAD_DIM = 64
B_HEADS = 4
B_Q_RANK = 256
B_KV_RANK = 128
B_NOPE_DIM = 64
B_ROPE_DIM = 32
B_V_DIM = 64
ROPE_THETA = 10000.0
C_HEADS = 4
NA_ROWS = 8
NA_COLS = 16
D_Q_HEADS = 4
D_KV_HEADS = 2
D_RADIUS = 128
T5_BUCKETS = 32
T5_MAX_DIST = 1024
T5_HEADS = N_DIL * A_HEADS + D_Q_HEADS
D_FF = 2816
CONV_W = 3

A_COLS = 3 * N_DIL * A_HEADS * HEAD_DIM
C_COLS = 3 * C_HEADS * HEAD_DIM

LANES = 128
SUBLANES = 8
VMEM_LIMIT_BYTES = 56 * 1024 * 1024

QKV_COLS = 3 * BRANCH_W
MAIN_A0 = 0
MAIN_C = MAIN_A0 + QKV_COLS
MAIN_DQ = MAIN_C + QKV_COLS
MAIN_DK = MAIN_DQ + 256
MAIN_DV = MAIN_DK + 128
MAIN_COLS = MAIN_DV + 128
DIL_STREAMS = tuple(dil for _, dil in DIL_CFG[1:])
DIL_BASE = tuple(MAIN_COLS + i * QKV_COLS for i in range(len(DIL_STREAMS)))
LAT_BASE = MAIN_COLS + len(DIL_STREAMS) * QKV_COLS
LAT_CQ = 0
LAT_CKV = 256
LAT_KR = 384
LAT_KRS = 512
LAT_COLS = 640

MLA_HEAD_PAD = 128

BF16 = jnp.bfloat16
F32 = jnp.float32
LOG2E = math.log2(math.e)
QK_SCALE = HEAD_DIM ** -0.5
assert math.frexp(QK_SCALE)[0] == 0.5
MLA_LOGIT_SCALE = (B_NOPE_DIM + B_ROPE_DIM) ** -0.5 * LOG2E


def _rms(x, g):
    return x * lax.rsqrt(jnp.mean(x * x, axis=-1, keepdims=True) + RMS_EPS) * g


def _dot(a, b):
    return jnp.dot(a, b, preferred_element_type=F32)


def _dot_nt(a, b):
    return lax.dot_general(a, b, (((1,), (1,)), ((), ())), preferred_element_type=F32)


def _params(*sem):
    return pltpu.CompilerParams(dimension_semantics=sem, vmem_limit_bytes=VMEM_LIMIT_BYTES)


def _resident(shape):
    nd = len(shape)
    return pl.BlockSpec(shape, lambda *_: (0,) * nd, pipeline_mode=pl.Buffered(1))


def _proj_in_kernel(x_ref, g_ref, w_ref, ct_ref, st_ref, gq_ref, gkv_ref, wq1_ref, wq2_ref, wk_ref, wv_ref,
                    main_ref, *rest, tm, chunk):
    nd = len(DIL_STREAMS)
    dil_refs, (q_ref, k_ref, vt_ref), scr_refs = rest[:nd], rest[nd:nd + 3], rest[nd + 3:]
    h = _rms(x_ref[...], g_ref[...]).astype(BF16)
    for c0 in range(0, MAIN_COLS, chunk):
        main_ref[:, c0:c0 + chunk] = _dot(h, w_ref[:, c0:c0 + chunk]).astype(BF16)
    nslab = QKV_COLS // LANES
    for out_ref, scr_ref, base, dil in zip(dil_refs, scr_refs, DIL_BASE, DIL_STREAMS):
        res = _dot(h, w_ref[:, base:base + QKV_COLS])
        for k in range(nslab):
            scr_ref[k] = res[:, k * LANES:(k + 1) * LANES]
        for r in range(dil):
            rows = [scr_ref[k, pl.ds(r, tm // dil, stride=dil), :] for k in range(nslab)]
            out_ref[0, r] = jnp.concatenate(rows, axis=-1).astype(BF16)

    lat = _dot(h, w_ref[:, LAT_BASE:])
    ct1, st1 = ct_ref[...], st_ref[...]
    ct = jnp.concatenate([ct1] * B_HEADS, axis=-1)
    st = jnp.concatenate([st1] * B_HEADS, axis=-1)
    cn = _rms(lat[:, LAT_CQ:LAT_CQ + B_Q_RANK], gq_ref[...]).astype(BF16)
    q = _dot(cn, wq1_ref[...]) * ct + _dot(cn, wq2_ref[...]) * st
    q_ref[...] = (q * MLA_LOGIT_SCALE).astype(BF16)
    kvn = _rms(lat[:, LAT_CKV:LAT_CKV + B_KV_RANK], gkv_ref[...]).astype(BF16)
    krope = lat[:, LAT_KR:LAT_KR + LANES] * ct1 + lat[:, LAT_KRS:LAT_KRS + LANES] * st1
    k = _dot(kvn, wk_ref[...]) + jnp.concatenate([krope] * B_HEADS, axis=-1)
    k_ref[...] = k.astype(BF16)
    vt_ref[0, 0] = _dot(kvn, wv_ref[...]).T.astype(BF16)


def _proj_in(x2, g, w, ct, st, gq, gkv, wq1, wq2, wk, wv, nb, seq, tm):
    n = x2.shape[0]
    tps = seq // tm
    hw = B_HEADS * MLA_HEAD_PAD
    vw = B_HEADS * B_V_DIM
    row = lambda width: pl.BlockSpec((tm, width), lambda i: (i, 0))
    tab = pl.BlockSpec((tm, LANES), lambda i: (i % tps, 0))
    dil_specs = [pl.BlockSpec((1, d, tm // d, QKV_COLS), lambda i: (i // tps, 0, i % tps, 0)) for d in DIL_STREAMS]
    dil_shapes = [jax.ShapeDtypeStruct((nb, d, seq // d, QKV_COLS), BF16) for d in DIL_STREAMS]
    small = [gq, gkv, wq1, wq2, wk, wv]
    outs = pl.pallas_call(
        functools.partial(_proj_in_kernel, tm=tm, chunk=512),
        grid=(n // tm,),
        in_specs=[row(D_MODEL), _resident((1, D_MODEL)), _resident(w.shape), tab, tab]
                 + [_resident(a.shape) for a in small],
        out_specs=[row(MAIN_COLS)] + dil_specs
                  + [row(hw), row(hw), pl.BlockSpec((1, 1, vw, tm), lambda i: (i // tps, i % tps, 0, 0))],
        out_shape=[jax.ShapeDtypeStruct((n, MAIN_COLS), BF16)] + dil_shapes
                  + [jax.ShapeDtypeStruct((n, hw), BF16)] * 2 + [jax.ShapeDtypeStruct((nb, tps, vw, tm), BF16)],
        scratch_shapes=[pltpu.VMEM((QKV_COLS // LANES, tm, LANES), F32) for _ in DIL_STREAMS],
        compiler_params=_params("parallel"),
        name="proj_in",
    )(x2, g, w, ct, st, *small)
    nd = len(DIL_STREAMS)
    return outs[0], outs[1:1 + nd], outs[1 + nd:]


def _pair_rows(qv, lo):
    zero = jnp.zeros_like(qv)
    return jnp.concatenate([jnp.where(lo, qv, zero), jnp.where(lo, zero, qv)], axis=0)


def _pair_softmax(s, sink):
    m = jnp.max(s, axis=-1, keepdims=True)
    if sink is not None:
        m = jnp.maximum(m, sink)
    p = jnp.exp2((s - m).astype(BF16))
    return p, m, (jnp.exp2(sink - m) if sink is not None else None)


def _pair_values(p, v, l_sink):
    res = _dot(p, jnp.concatenate([v, jnp.ones_like(v)], axis=-1))
    l = res[:, LANES:]
    if l_sink is not None:
        l = l + l_sink
    return res[:, :LANES] / l, l


def _band_kernel(*refs, seq, tq, radius, npair, nkv, has_sink, emit_stats):
    refs = list(refs)
    bias_ref = refs.pop(0)
    sink_ref = refs.pop(0) if has_sink else None
    q_ref, k_ref, v_ref, o_ref = refs[:4]
    st_ref = refs[4] if emit_stats else None
    win = tq + 2 * radius
    nblk = seq // tq
    lane = lax.broadcasted_iota(jnp.int32, (tq, LANES), 1)
    lo = lane < HEAD_DIM
    top = lax.broadcasted_iota(jnp.int32, (2 * tq, 1), 0) < tq

    nsl = q_ref.shape[0]
    ub = min(nsl * nblk, 8)
    assert (nsl * nblk) % ub == 0
    units = [(u, j) for u in range(ub) for j in range(npair)]
    kv_lanes = [slice((j if nkv == npair else 0) * LANES, ((j if nkv == npair else 0) + 1) * LANES)
                for j in range(npair)]

    def body(step, carry):
        flat = [step * ub + u for u in range(ub)]
        sl = [lax.div(f, nblk) for f in flat]
        blk = [lax.rem(f, nblk) for f in flat]
        q0 = [pl.multiple_of(b * tq, tq) for b in blk]
        ks = [pl.multiple_of(jnp.clip(q - radius, 0, seq - win), radius) for q in q0]
        variant = [jnp.where(b == 0, 0, jnp.where(b == nblk - 1, 2, 1)) for b in blk]
        s = {}
        for u, j in units:
            q2 = _pair_rows(q_ref[sl[u], pl.ds(q0[u], tq), j * LANES:(j + 1) * LANES], lo)
            s[u, j] = _dot_nt(q2, k_ref[sl[u], pl.ds(ks[u], win), kv_lanes[j]]) * LOG2E + bias_ref[variant[u], j]
        soft = {}
        for u, j in units:
            sink = jnp.where(top, sink_ref[2 * j], sink_ref[2 * j + 1]) if has_sink else None
            soft[u, j] = _pair_softmax(s[u, j], sink)
        for u in range(ub):
            outs = []
            stats = jnp.zeros((tq, LANES), F32)
            for j in range(npair):
                p, m, l_sink = soft[u, j]
                pv, l = _pair_values(p, v_ref[sl[u], pl.ds(ks[u], win), kv_lanes[j]], l_sink)
                outs.append(jnp.where(lo, pv[:tq], pv[tq:]))
                if emit_stats:
                    for half, rows in enumerate((slice(0, tq), slice(tq, 2 * tq))):
                        stats = jnp.where(lane == 2 * j + half, m[rows], stats)
                        stats = jnp.where(lane == 2 * npair + 2 * j + half, l[rows], stats)
            o_ref[sl[u], pl.ds(q0[u], tq), :] = jnp.concatenate(outs, axis=-1).astype(o_ref.dtype)
            if emit_stats:
                st_ref[sl[u], pl.ds(q0[u], tq), :] = stats
        return carry

    lax.fori_loop(0, nsl * nblk // ub, body, 0)


def _band_attention(bias, sink, src, *, nstream, seq, cols, qcol, kcol, vcol, kvw, radius,
                    out_dtype, emit_stats, name):
    nb = src.shape[0]
    npair = bias.shape[1]
    tq = bias.shape[2] // 2
    qw = npair * LANES
    assert seq % tq == 0 and seq >= tq + 2 * radius and tq >= radius and bias.shape[3] == tq + 2 * radius
    kernel = functools.partial(_band_kernel, seq=seq, tq=tq, radius=radius, npair=npair, nkv=kvw // LANES,
                               has_sink=sink is not None, emit_stats=emit_stats)
    in_specs = [_resident(bias.shape)]
    args = [bias]
    if sink is not None:
        in_specs.append(pl.BlockSpec(memory_space=pltpu.SMEM))
        args.append(sink)
    nsl = max(1, min(nb, 1024 // seq))
    assert nb % nsl == 0
    in_specs += [
        pl.BlockSpec((nsl, seq, qw), lambda b, r: (b, 0, r * (cols // qw) + qcol)),
        pl.BlockSpec((nsl, seq, kvw), lambda b, r: (b, 0, r * (cols // kvw) + kcol)),
        pl.BlockSpec((nsl, seq, kvw), lambda b, r: (b, 0, r * (cols // kvw) + vcol)),
    ]
    args += [src, src, src]
    out_specs = [pl.BlockSpec((nsl, seq, qw), lambda b, r: (b, 0, r))]
    out_shape = [jax.ShapeDtypeStruct((nb, seq, nstream * qw), out_dtype)]
    if emit_stats:
        out_specs.append(pl.BlockSpec((nsl, seq, LANES), lambda b, r: (b, 0, r)))
        out_shape.append(jax.ShapeDtypeStruct((nb, seq, nstream * LANES), F32))
    outs = pl.pallas_call(
        kernel,
        grid=(nb // nsl, nstream),
        in_specs=in_specs,
        out_specs=out_specs,
        out_shape=out_shape,
        compiler_params=_params("parallel", "parallel"),
        name=name,
    )(*args)
    return outs if emit_stats else outs[0]


def _t5_bucket(rel):
    half = T5_BUCKETS // 2
    exact = half // 2
    n = np.abs(rel)
    large = exact + (np.log(np.maximum(n, 1) / exact) / math.log(T5_MAX_DIST / exact)
                     * (half - exact)).astype(np.int32)
    large = np.minimum(large, half - 1)
    return (np.where(rel > 0, half, 0) + np.where(n < exact, n, large)).astype(np.int32)


def _band_bias(t5_table, heads, radius, dil, tq=128):
    win = tq + 2 * radius
    period = win + tq
    j = np.arange(period)
    offs = np.array([0, -radius, -2 * radius])
    rel = offs[:, None] + np.where(j < win, j, j - period)[None, :]
    valid = np.abs(rel) <= radius
    bucket = _t5_bucket(dil * np.clip(rel, -radius, radius))
    g = jnp.transpose(t5_table[:, np.asarray(heads)][bucket], (0, 2, 1)).astype(F32)
    g = jnp.where(valid[:, None, :], g, NEG_INF)
    flat = jnp.tile(g, (1, 1, tq))[:, :, :tq * (period - 1)]
    tiles = flat.reshape(3, len(heads), tq, period - 1)[:, :, :, :win]
    return tiles.reshape(3, len(heads) // 2, 2 * tq, win) * LOG2E


def _mla_attn_kernel(q_ref, k_ref, vt_ref, o_ref, *, nchunk, tq):
    heads = range(B_HEADS)
    q = [q_ref[0, :, h * MLA_HEAD_PAD:(h + 1) * MLA_HEAD_PAD] for h in heads]

    def scores(c):
        return [_dot_nt(k_ref[0, c, :, h * MLA_HEAD_PAD:(h + 1) * MLA_HEAD_PAD], q[h]) for h in heads]

    init = (jnp.full((1, tq), -jnp.inf, F32), jnp.zeros((1, tq), F32), jnp.zeros((B_V_DIM, tq), F32))
    carry = (init,) * B_HEADS
    s = scores(0)
    for c in range(nchunk):
        ahead = scores(c + 1) if c + 1 < nchunk else None
        stats = []
        for h in heads:
            m, l, _ = carry[h]
            m_new = jnp.maximum(m, jnp.max(s[h], axis=0, keepdims=True))
            alpha = jnp.exp2(m - m_new)
            p = jnp.exp2(s[h] - m_new)
            stats.append((m_new, alpha * l + jnp.sum(p, axis=0, keepdims=True), alpha, p.astype(BF16)))
        new = []
        for h in heads:
            m_new, l, alpha, p = stats[h]
            pv = _dot(vt_ref[0, c, h * B_V_DIM:(h + 1) * B_V_DIM, :], p)
            new.append((m_new, l, alpha * carry[h][2] + pv))
        carry = tuple(new)
        s = ahead
    out_t = jnp.concatenate([acc / l for _, l, acc in carry], axis=0)
    o_ref[0] = out_t.T.astype(o_ref.dtype)


def _mla_attn(q, k, vt, tq):
    nb, seq, hw = q.shape
    nchunk = k.shape[1]
    return pl.pallas_call(
        functools.partial(_mla_attn_kernel, nchunk=nchunk, tq=tq),
        grid=(nb, seq // tq),
        in_specs=[
            pl.BlockSpec((1, tq, hw), lambda b, i: (b, i, 0)),
            pl.BlockSpec((1,) + k.shape[1:], lambda b, i: (b, 0, 0, 0)),
            pl.BlockSpec((1,) + vt.shape[1:], lambda b, i: (b, 0, 0, 0)),
        ],
        out_specs=pl.BlockSpec((1, tq, BRANCH_W), lambda b, i: (b, i, 0)),
        out_shape=jax.ShapeDtypeStruct((nb, seq, BRANCH_W), BF16),
        compiler_params=_params("parallel", "parallel"),
        name="mla_attn",
    )(q, k, vt)


def _rope_tables(seq):
    half = B_ROPE_DIM // 2
    inv = ROPE_THETA ** (-jnp.arange(half, dtype=F32) / half)
    ang = jnp.arange(seq, dtype=F32)[:, None] * inv[None, :]
    cos, sin = jnp.cos(ang), jnp.sin(ang)
    pad = jnp.zeros((seq, MLA_HEAD_PAD - B_NOPE_DIM - B_ROPE_DIM), F32)
    ct = jnp.concatenate([jnp.ones((seq, B_NOPE_DIM), F32), cos, cos, pad], axis=-1)
    st = jnp.concatenate([jnp.zeros((seq, B_NOPE_DIM), F32), -sin, sin, pad], axis=-1)
    return ct, st


def _mla_weights(w_uq, w_ukv):
    half = B_ROPE_DIM // 2
    wq = w_uq.reshape(B_Q_RANK, B_HEADS, B_NOPE_DIM + B_ROPE_DIM)
    nope, r1, r2 = wq[..., :B_NOPE_DIM], wq[..., B_NOPE_DIM:B_NOPE_DIM + half], wq[..., B_NOPE_DIM + half:]
    zpad = jnp.zeros((B_Q_RANK, B_HEADS, MLA_HEAD_PAD - B_NOPE_DIM - B_ROPE_DIM), F32)
    wq1 = jnp.concatenate([nope, r1, r2, zpad], axis=-1).reshape(B_Q_RANK, -1)
    wq2 = jnp.concatenate([jnp.zeros_like(nope), r2, r1, zpad], axis=-1).reshape(B_Q_RANK, -1)
    wkv = w_ukv.reshape(B_KV_RANK, B_HEADS, B_NOPE_DIM + B_V_DIM)
    z64 = jnp.zeros((B_KV_RANK, B_HEADS, MLA_HEAD_PAD - B_NOPE_DIM), F32)
    wk = jnp.concatenate([wkv[..., :B_NOPE_DIM], z64], axis=-1).reshape(B_KV_RANK, -1)
    wv = wkv[..., B_NOPE_DIM:].reshape(B_KV_RANK, -1)
    return tuple(w.astype(BF16) for w in (wq1, wq2, wk, wv))


def _na_kernel(bias_ref, q_ref, k_ref, v_ref, o_ref, *, rows):
    kh = NA_ROWS
    win = kh * GRID_W
    tq = GRID_W
    lo = lax.broadcasted_iota(jnp.int32, (tq, LANES), 1) < HEAD_DIM

    ub = 8
    npair = C_HEADS // 2
    units = [(u, j) for u in range(ub) for j in range(npair)]
    lanes = [slice(j * LANES, (j + 1) * LANES) for j in range(npair)]

    def body(step, carry):
        r = [step * ub + u for u in range(ub)]
        q0 = [pl.multiple_of(ri * tq, tq) for ri in r]
        r0 = [jnp.clip(ri - kh // 2, 0, rows - kh) for ri in r]
        ks = [pl.multiple_of(x * GRID_W, GRID_W) for x in r0]
        s = {}
        for u, j in units:
            q2 = _pair_rows(q_ref[0, pl.ds(q0[u], tq), lanes[j]], lo)
            s[u, j] = _dot_nt(q2, k_ref[0, pl.ds(ks[u], win), lanes[j]]) * LOG2E + bias_ref[r[u] - r0[u], j]
        soft = {uj: _pair_softmax(s[uj], None) for uj in units}
        for u in range(ub):
            outs = []
            for j in range(npair):
                p, _, _ = soft[u, j]
                pv, _ = _pair_values(p, v_ref[0, pl.ds(ks[u], win), lanes[j]], None)
                outs.append(jnp.where(lo, pv[:tq], pv[tq:]))
            o_ref[0, pl.ds(q0[u], tq), :] = jnp.concatenate(outs, axis=-1).astype(o_ref.dtype)
        return carry

    assert rows % ub == 0
    lax.fori_loop(0, rows // ub, body, 0)


def _na_attention(bias, src, seq, qcol):
    nb = src.shape[0]
    rows = seq // GRID_W
    assert rows >= NA_ROWS
    spec = lambda c: pl.BlockSpec((1, seq, BRANCH_W), lambda b: (b, 0, c))
    return pl.pallas_call(
        functools.partial(_na_kernel, rows=rows),
        grid=(nb,),
        in_specs=[_resident(bias.shape), spec(qcol), spec(qcol + 1), spec(qcol + 2)],
        out_specs=pl.BlockSpec((1, seq, BRANCH_W), lambda b: (b, 0, 0)),
        out_shape=jax.ShapeDtypeStruct((nb, seq, BRANCH_W), BF16),
        compiler_params=_params("parallel"),
        name="na_attn",
    )(bias, src, src, src)


def _na_bias(rpb):
    kh = NA_ROWS
    qc = np.arange(GRID_W)[:, None]
    kc = np.arange(GRID_W)[None, :]
    start = np.clip(qc - NA_COLS // 2, 0, GRID_W - NA_COLS)
    col_ok = (kc >= start) & (kc < start + NA_COLS)
    heads = rpb.shape[0]
    period = 2 * GRID_W
    j = np.arange(period)
    d = np.where(j < GRID_W, j, j - period)
    g = jnp.where(np.abs(d) < NA_COLS, rpb[:, :, np.clip(d + NA_COLS - 1, 0, 2 * NA_COLS - 2)], 0.0)
    flat = jnp.tile(g.astype(F32), (1, 1, GRID_W))[:, :, :GRID_W * (period - 1)]
    tiles = flat.reshape(heads, 2 * NA_ROWS - 1, GRID_W, period - 1)[..., :GRID_W]
    b = jnp.stack([tiles[:, NA_ROWS - 1 - v:2 * NA_ROWS - 1 - v] for v in range(kh)])
    b = jnp.where(col_ok[None, None, None], b, NEG_INF)
    b = jnp.transpose(b, (0, 1, 3, 2, 4))
    return b.reshape(kh, heads // 2, 2 * GRID_W, kh * GRID_W) * LOG2E


def _merge_kernel(x_ref, g_ref, oa0_ref, oa1_ref, oa2_ref, sa0_ref, sa1_ref, sa2_ref, yb_ref, yc_ref, yd_ref,
                  wg_ref, wb_ref, wo_ref, out_ref, *scr_refs, tm):
    x = x_ref[...]
    h = _rms(x, g_ref[...]).astype(BF16)

    def token_order(ref, scr_ref, dil):
        nslab = scr_ref.shape[0]
        for r in range(dil):
            blk = ref[0, r]
            for k in range(nslab):
                scr_ref[k, pl.ds(r, tm // dil, stride=dil), :] = blk[:, k * LANES:(k + 1) * LANES].astype(F32)
        return jnp.concatenate([scr_ref[k] for k in range(nslab)], axis=-1)

    o = [oa0_ref[...].astype(F32)] + [token_order(ref, scr, d)
                                      for ref, scr, d in zip((oa1_ref, oa2_ref), scr_refs[:2], DIL_STREAMS)]
    st = [sa0_ref[...]] + [token_order(ref, scr, d) for ref, scr, d in zip((sa1_ref, sa2_ref), scr_refs[2:], DIL_STREAMS)]
    m = [s[:, 0:A_HEADS] for s in st]
    l = [s[:, A_HEADS:2 * A_HEADS] for s in st]
    m_all = jnp.maximum(jnp.maximum(m[0], m[1]), m[2])
    wgt = [l[g] * jnp.exp2(m[g] - m_all) for g in range(N_DIL)]
    den = wgt[0] + wgt[1] + wgt[2]
    parts = []
    for hh in range(A_HEADS):
        hs = slice(hh * HEAD_DIM, (hh + 1) * HEAD_DIM)
        num = wgt[0][:, hh:hh + 1] * o[0][:, hs]
        for g in range(1, N_DIL):
            num = num + wgt[g][:, hh:hh + 1] * o[g][:, hs]
        parts.append(num / den[:, hh:hh + 1])
    ya = jnp.concatenate(parts, axis=-1).astype(BF16)

    branches = [ya, yb_ref[...], yc_ref[...], yd_ref[...]]
    order = (1, 2, 3, 0)
    logits = _dot(h, wg_ref[order[0]])
    merged = None
    for n, i in enumerate(order):
        ahead = _dot(h, wg_ref[order[n + 1]]) if n + 1 < N_BRANCH else None
        term = jax.nn.sigmoid(logits) * _dot(branches[i], wb_ref[i])
        merged = term if merged is None else merged + term
        logits = ahead
    out_ref[...] = x + _dot(merged.astype(BF16), wo_ref[...])


def _merge(x2, g, oa, sa, yb, yc, yd, wg, wb, wo, seq, tm):
    n = x2.shape[0]
    tps = seq // tm
    row = lambda w: pl.BlockSpec((tm, w), lambda i: (i, 0))
    streams = lambda w: [pl.BlockSpec((1, d, tm // d, w), lambda i: (i // tps, 0, i % tps, 0)) for d in DIL_STREAMS]
    return pl.pallas_call(
        functools.partial(_merge_kernel, tm=tm),
        grid=(n // tm,),
        in_specs=[row(D_MODEL), _resident(g.shape), row(BRANCH_W)] + streams(BRANCH_W) + [row(LANES)]
                 + streams(LANES) + [row(BRANCH_W)] * 3
                 + [_resident(wg.shape), _resident(wb.shape), _resident(wo.shape)],
        out_specs=row(D_MODEL),
        out_shape=jax.ShapeDtypeStruct((n, D_MODEL), F32),
        scratch_shapes=[pltpu.VMEM((BRANCH_W // LANES, tm, LANES), F32) for _ in DIL_STREAMS]
                       + [pltpu.VMEM((1, tm, LANES), F32) for _ in DIL_STREAMS],
        compiler_params=_params("parallel"),
        name="merge",
    )(x2, g, *oa, *sa, yb, yc, yd, wg, wb, wo)


def _gelu_tanh(x):
    c = math.sqrt(2.0 / math.pi)
    half = 0.5 * x
    return half + half * jnp.tanh(x * (c + (c * 0.044715) * (x * x)))


def _ffn_kernel(*refs, tm, tiles_per_seq, fc, final):
    refs = list(refs)
    xp_ref, x_ref, xn_ref, g_ref, wg_ref, wu_ref, cw_ref, cb_ref, wd_ref = refs[:9]
    fg_ref = refs[9] if final else None
    out_ref, u_ref = refs[-2:]
    nchunk = D_FF // fc
    i = pl.program_id(0)
    t = i % tiles_per_seq
    keep_prev = (t != 0).astype(F32)
    keep_next = (t != tiles_per_seq - 1).astype(F32)
    x = x_ref[...]
    g = g_ref[...]
    h = _rms(x, g).astype(BF16)
    hp = _rms(xp_ref[...], g).astype(BF16)
    hn = _rms(xn_ref[...], g).astype(BF16)
    row = lax.broadcasted_iota(jnp.int32, (tm, 1), 0)

    def project(c):
        cs = slice(c * fc, (c + 1) * fc)
        wg = wg_ref[:, cs]
        gp = _dot(hp, wg)[SUBLANES - 1:SUBLANES, :] * keep_prev
        gn = _dot(hn, wg)[0:1, :] * keep_next
        return _dot(h, wg), gp, gn, _dot(h, wu_ref[:, cs])

    def mix(c, gate, gp, gn, up):
        cs = slice(c * fc, (c + 1) * fc)
        prev = jnp.where(row == 0, gp, pltpu.roll(gate, 1, axis=0))
        nxt = jnp.where(row == tm - 1, gn, pltpu.roll(gate, tm - 1, axis=0))
        conv = cw_ref[0:1, cs] * prev + cw_ref[1:2, cs] * gate + cw_ref[2:3, cs] * nxt + cb_ref[:, cs]
        u_ref[:, cs] = (_gelu_tanh(conv) * up).astype(BF16)

    cur = project(0)
    for c in range(nchunk):
        ahead = project(c + 1) if c + 1 < nchunk else None
        mix(c, *cur)
        cur = ahead
    y = x + _dot(u_ref[...], wd_ref[...])
    if final:
        y = _rms(y, fg_ref[...])
    out_ref[...] = y


def _ffn(x2, g, wg, wu, cw, cb, wd, final_g, seq, tm):
    n = x2.shape[0]
    fc = 256
    assert D_FF % fc == 0
    nb8 = n // SUBLANES
    step = tm // SUBLANES
    final = final_g is not None
    in_specs = [
        pl.BlockSpec((SUBLANES, D_MODEL), lambda i: (jnp.maximum(i * step - 1, 0), 0)),
        pl.BlockSpec((tm, D_MODEL), lambda i: (i, 0)),
        pl.BlockSpec((SUBLANES, D_MODEL), lambda i: (jnp.minimum((i + 1) * step, nb8 - 1), 0)),
        _resident(g.shape), _resident(wg.shape), _resident(wu.shape), _resident(cw.shape),
        _resident(cb.shape), _resident(wd.shape),
    ]
    args = [x2, x2, x2, g, wg, wu, cw, cb, wd]
    if final:
        in_specs.append(_resident(final_g.shape))
        args.append(final_g)
    return pl.pallas_call(
        functools.partial(_ffn_kernel, tm=tm, tiles_per_seq=seq // tm, fc=fc, final=final),
        grid=(n // tm,),
        in_specs=in_specs,
        out_specs=pl.BlockSpec((tm, D_MODEL), lambda i: (i, 0)),
        out_shape=jax.ShapeDtypeStruct((n, D_MODEL), F32),
        scratch_shapes=[pltpu.VMEM((tm, D_FF), BF16)],
        compiler_params=_params("parallel"),
        name="ffn",
    )(*args)


def _proj_in_weight(w):
    off = np.concatenate([[0], np.cumsum([A_COLS, B_Q_RANK, B_KV_RANK, B_ROPE_DIM, C_COLS,
                                          D_Q_HEADS * HEAD_DIM, D_KV_HEADS * HEAD_DIM])])
    a0, cq0, ckv0, kr0, c0, dq0, dk0, dv0 = (int(o) for o in off)
    half = B_ROPE_DIM // 2
    pad = LANES - B_NOPE_DIM - B_ROPE_DIM
    qkv = lambda base, step: [(base, BRANCH_W, QK_SCALE), (base + step, BRANCH_W, 1.0), (base + 2 * step, BRANCH_W, 1.0)]
    group = lambda g: qkv(a0 + g * BRANCH_W, N_DIL * BRANCH_W)
    segs = (group(0) + qkv(c0, BRANCH_W)
            + [(dq0 + h * HEAD_DIM, HEAD_DIM, QK_SCALE) for h in D_HEAD_ORDER]
            + [(dk0, D_KV_HEADS * HEAD_DIM, 1.0), (dv0, D_KV_HEADS * HEAD_DIM, 1.0)]
            + [s for g in range(1, N_DIL) for s in group(g)]
            + [(cq0, B_Q_RANK, 1.0), (ckv0, B_KV_RANK, 1.0)]
            + [(None, B_NOPE_DIM, 0.0), (kr0, B_ROPE_DIM, 1.0), (None, pad, 0.0)]
            + [(None, B_NOPE_DIM, 0.0), (kr0 + half, half, 1.0), (kr0, half, 1.0), (None, pad, 0.0)])
    cols = []
    for start, width, scale in segs:
        if start is None:
            cols.append(jnp.zeros((D_MODEL, width), w.dtype))
        else:
            piece = w[:, start:start + width]
            cols.append(piece if scale == 1.0 else piece * scale)
    out = jnp.concatenate(cols, axis=1).astype(BF16)
    assert out.shape[1] == LAT_BASE + LAT_COLS
    return out


D_HEAD_ORDER = (0, 2, 1, 3)


def _permute_heads(w, order, axis):
    return jnp.concatenate([lax.slice_in_dim(w, h * HEAD_DIM, (h + 1) * HEAD_DIM, axis=axis) for h in order],
                           axis=axis)


def _layer(x2, nb, seq, t5_table, d_bias, ct, st, p, final_g):
    n = x2.shape[0]
    tm = 512
    main, dilated, (q, k, vt) = _proj_in(x2, p["norm_mix_g"], p["w_in"], ct, st, p["q_norm_g"], p["kv_norm_g"],
                                         *p["mla_w"], nb, seq, tm)
    main = main.reshape(nb, seq, MAIN_COLS)

    oa, sa = [], []
    for gi, (window, dil) in enumerate(DIL_CFG):
        radius = window // dil // 2
        bias = _band_bias(t5_table, range(gi * A_HEADS, (gi + 1) * A_HEADS), radius, dil)
        src = main if gi == 0 else dilated[gi - 1].reshape(nb * dil, seq // dil, QKV_COLS)
        o, s = _band_attention(bias, None, src, nstream=1, seq=seq // dil, cols=src.shape[-1],
                               qcol=0, kcol=1, vcol=2, kvw=BRANCH_W, radius=radius,
                               out_dtype=BF16, emit_stats=True, name=f"band_a{gi}")
        if gi == 0:
            oa.append(o.reshape(n, BRANCH_W))
            sa.append(s.reshape(n, LANES))
        else:
            oa.append(o.reshape(nb, dil, seq // dil, BRANCH_W))
            sa.append(s.reshape(nb, dil, seq // dil, LANES))

    hw =B_HEADS * MLA_HEAD_PAD
    yb = _mla_attn(q.reshape(nb, seq, hw), k.reshape(nb, seq // tm, tm, hw), vt, 512)

    yc = _na_attention(_na_bias(p["na_bias"]), main, seq, MAIN_C // BRANCH_W)

    yd = _band_attention(d_bias, p["sink_logit"], main, nstream=1, seq=seq, cols=MAIN_COLS,
                         qcol=MAIN_DQ // BRANCH_W, kcol=MAIN_DK // LANES, vcol=MAIN_DV // LANES, kvw=LANES,
                         radius=D_RADIUS, out_dtype=BF16, emit_stats=False, name="band_d")

    x2 = _merge(x2, p["norm_mix_g"], oa, sa, yb.reshape(n, BRANCH_W), yc.reshape(n, BRANCH_W),
                yd.reshape(n, BRANCH_W), p["w_gate"], p["w_branch"], p["w_out"], seq, tm)
    return _ffn(x2, p["norm_ffn_g"], p["w_ffn_gate"], p["w_ffn_up"], p["conv_w"], p["conv_b"],
                p["w_ffn_down"], final_g, seq, 2 * tm)


def kernel(x, t5_table, norm_mix_g, w_in, q_norm_g, w_uq, kv_norm_g, w_ukv, na_bias, sink_logit, w_gate, w_branch, w_out, norm_ffn_g, w_ffn_gate, w_ffn_up, conv_w, conv_b, w_ffn_down, final_g):
    nb, seq, _ = x.shape
    depth = w_in.shape[0]
    x2 =x.reshape(nb * seq, D_MODEL)
    ct, st = _rope_tables(seq)
    d_bias = _band_bias(t5_table, [N_DIL * A_HEADS + h for h in D_HEAD_ORDER], D_RADIUS, 1)
    w_gate, w_out, w_ffn_gate, w_ffn_up, w_ffn_down = (
        w.astype(BF16) for w in (w_gate, w_out, w_ffn_gate, w_ffn_up, w_ffn_down))
    for layer in range(depth):
        wb = w_branch[layer]
        wb = jnp.concatenate([wb[:3], _permute_heads(wb[3], D_HEAD_ORDER, axis=0)[None]], axis=0)
        p = {
            "norm_mix_g": norm_mix_g[layer][None, :],
            "w_in": _proj_in_weight(w_in[layer]),
            "q_norm_g": q_norm_g[layer][None, :],
            "kv_norm_g": kv_norm_g[layer][None, :],
            "mla_w": _mla_weights(w_uq[layer], w_ukv[layer]),
            "na_bias": na_bias[layer],
            "sink_logit": sink_logit[layer][np.asarray(D_HEAD_ORDER)] * LOG2E,
            "w_gate": w_gate[layer],
            "w_branch": wb.astype(BF16),
            "w_out": w_out[layer],
            "norm_ffn_g": norm_ffn_g[layer][None, :],
            "w_ffn_gate": w_ffn_gate[layer],
            "w_ffn_up": w_ffn_up[layer],
            "conv_w": conv_w[layer],
            "conv_b": conv_b[layer][None, :],
            "w_ffn_down": w_ffn_down[layer],
        }
        x2 = _layer(x2, nb, seq, t5_table, d_bias, ct, st, p,
                    final_g[None, :] if layer == depth - 1 else None)
    return x2.reshape(nb, seq, D_MODEL)
```
